```python
import jax, jax.numpy as jnp
from jax import lax
import numpy as np

D_MODEL = 1024
BATCH = 8
SEQ = 8192
DEPTH = 2

CHUNK = 64
CONV_W = 4
EPS = 1e-6
F32 = jnp.float32

GDN_HEADS = 4
GDN_DK = 128
GDN_DV = 128
GDN_W = GDN_HEADS * GDN_DV
SSD_HEADS = 16
SSD_P = 64
SSD_N = 128
SSD_GROUPS = 2
SSD_W = SSD_HEADS * SSD_P
RET_HEADS = 4
RET_DK = 128
RET_DV = 128
RET_W = RET_HEADS * RET_DV
ROPE_BASE = 10000.0

MIX_W = GDN_W + SSD_W + RET_W

GDN_SIZES = [GDN_HEADS * GDN_DK, GDN_HEADS * GDN_DK, GDN_W, GDN_W, GDN_HEADS, GDN_HEADS]
SSD_SIZES = [SSD_W, SSD_GROUPS * SSD_N, SSD_GROUPS * SSD_N, SSD_W, SSD_HEADS]
RET_SIZES = [RET_HEADS * RET_DK, RET_HEADS * RET_DK, RET_W, RET_W]
IN_SIZES = GDN_SIZES + SSD_SIZES + RET_SIZES
N_IN = sum(IN_SIZES)
GDN_CONV_CH = 2 * GDN_HEADS * GDN_DK + GDN_W
SSD_CONV_CH = SSD_W + 2 * SSD_GROUPS * SSD_N

kernel_name = "hybrid_gdn_ssd_retention_parallel_groups"


def rmsnorm(x, w):
    xf = x.astype(F32)
    return xf * lax.rsqrt(jnp.mean(xf * xf, axis=-1, keepdims=True) + EPS) * w.astype(F32)


def split_cols(t, sizes):
    out, start = [], 0
    for s in sizes:
        out.append(t[..., start:start + s])
        start += s
    return out


def causal_conv(x, w):
    k = w.shape[0]
    seq = x.shape[1]
    xp = jnp.pad(x, ((0, 0), (k - 1, 0), (0, 0)))
    return sum(xp[:, i:i + seq] * w[i].astype(F32) for i in range(k))


def to_chunks(t):
    b, l = t.shape[:2]
    return jnp.moveaxis(t.reshape(b, l // CHUNK, CHUNK, *t.shape[2:]), 1, 0)


def from_chunks(t):
    n, b, c = t.shape[:3]
    return jnp.moveaxis(t, 0, 1).reshape(b, n * c, *t.shape[3:])


def l2norm(t):
    return t * lax.rsqrt(jnp.sum(t * t, axis=-1, keepdims=True) + EPS)


def rotary(t, pos):
    half = t.shape[-1] // 2
    inv = ROPE_BASE ** (-jnp.arange(half, dtype=F32) / half)
    ang = pos.astype(F32)[:, None] * inv[None, :]
    cos = jnp.cos(ang)[None, :, None, :]
    sin = jnp.sin(ang)[None, :, None, :]
    t1, t2 = t[..., :half], t[..., half:]
    return jnp.concatenate([t1 * cos - t2 * sin, t1 * sin + t2 * cos], axis=-1)


def gated_deltanet(q, k, v, b_raw, a_raw, A_log, dt_bias):
    bsz = q.shape[0]
    q = l2norm(q) * (GDN_DK ** -0.5)
    k = l2norm(k)
    beta = jax.nn.sigmoid(b_raw)
    g = -jnp.exp(A_log.astype(F32)) * jax.nn.softplus(a_raw + dt_bias.astype(F32))
    causal = jnp.tril(jnp.ones((CHUNK, CHUNK), bool))
    strict = jnp.tril(jnp.ones((CHUNK, CHUNK), F32), -1)
    eye = jnp.eye(CHUNK, dtype=F32)

    def step(S, inp):
        qi, ki, vi, bi, gi = inp
        gcum = jnp.cumsum(gi, axis=1)
        gh = jnp.swapaxes(gcum, 1, 2)
        diff = gh[..., :, None] - gh[..., None, :]
        decay = jnp.exp(jnp.where(causal, diff, -jnp.inf))
        kb = ki * bi[..., None]
        a_low = jnp.einsum('bihd,bjhd->bhij', kb, ki) * decay * strict
        rhs = jnp.concatenate([vi * bi[..., None], kb * jnp.exp(gcum)[..., None]], axis=-1)
        rhs = jnp.swapaxes(rhs, 1, 2)
        sol = lax.linalg.triangular_solve(a_low + eye, rhs, left_side=True, lower=True)
        u, w = sol[..., :GDN_DV], sol[..., GDN_DV:]
        v_new = u - jnp.einsum('bhcd,bhdv->bhcv', w, S)
        attn = jnp.einsum('bihd,bjhd->bhij', qi, ki) * decay
        o = (jnp.einsum('bihd,bhdv->bihv', qi * jnp.exp(gcum)[..., None], S)
             + jnp.einsum('bhij,bhjv->bihv', attn, v_new))
        glast = gcum[:, -1]
        kdec = ki * jnp.exp(glast[:, None, :] - gcum)[..., None]
        S = S * jnp.exp(glast)[:, :, None, None] + jnp.einsum('bjhd,bhjv->bhdv', kdec, v_new)
        return S, o

    s0 = jnp.zeros((bsz, GDN_HEADS, GDN_DK, GDN_DV), F32)
    _, o = lax.scan(step, s0, (to_chunks(q), to_chunks(k), to_chunks(v), to_chunks(beta), to_chunks(g)))
    return from_chunks(o)


def ssd(x, Bm, Cm, dt_raw, A_log, dt_bias, D):
    bsz, seq = x.shape[:2]
    hg = SSD_HEADS // SSD_GROUPS
    dt = jax.nn.softplus(dt_raw + dt_bias.astype(F32))
    a = (dt * -jnp.exp(A_log.astype(F32))).reshape(bsz, seq, SSD_GROUPS, hg)
    xdt = (x * dt[..., None]).reshape(bsz, seq, SSD_GROUPS, hg, SSD_P)
    causal = jnp.tril(jnp.ones((CHUNK, CHUNK), bool))

    def step(hs, inp):
        xi, bi, ci, ai = inp
        acum = jnp.cumsum(ai, axis=1)
        diff = acum[:, :, None] - acum[:, None]
        lmat = jnp.exp(jnp.where(causal[None, :, :, None, None], diff, -jnp.inf))
        cb = jnp.einsum('bign,bjgn->bijg', ci, bi)
        y = jnp.einsum('bijg,bijgh,bjghp->bighp', cb, lmat, xi)
        y = y + jnp.einsum('bign,bghpn->bighp', ci, hs) * jnp.exp(acum)[..., None]
        alast = acum[:, -1]
        wdec = jnp.exp(alast[:, None] - acum)
        hs = hs * jnp.exp(alast)[..., None, None] + jnp.einsum('bjgn,bjgh,bjghp->bghpn', bi, wdec, xi)
        return hs, y

    h0 = jnp.zeros((bsz, SSD_GROUPS, hg, SSD_P, SSD_N), F32)
    _, y = lax.scan(step, h0, (to_chunks(xdt), to_chunks(Bm), to_chunks(Cm), to_chunks(a)))
    y = from_chunks(y).reshape(bsz, seq, SSD_HEADS, SSD_P)
    return y + x * D.astype(F32)[:, None]


def retention(q, k, v):
    bsz = q.shape[0]
    lg = jnp.log(1.0 - 2.0 ** (-5.0 - jnp.arange(RET_HEADS, dtype=F32)))
    idx = jnp.arange(CHUNK, dtype=F32)
    rel = idx[:, None] - idx[None, :]
    dmat = jnp.where(rel[None] >= 0, jnp.exp(jnp.maximum(rel, 0.0)[None] * lg[:, None, None]), 0.0)
    qdec = jnp.exp((idx[:, None] + 1.0) * lg[None, :])
    kdec = jnp.exp((CHUNK - 1.0 - idx)[:, None] * lg[None, :])
    cdec = jnp.exp(CHUNK * lg)
    k = k * (RET_DK ** -0.5)

    def step(R, inp):
        qi, ki, vi = inp
        s = jnp.einsum('bihd,bjhd->bhij', qi, ki) * dmat
        o = (jnp.einsum('bhij,bjhv->bihv', s, vi)
             + jnp.einsum('bihd,bhdv->bihv', qi, R) * qdec[None, :, :, None])
        R = R * cdec[None, :, None, None] + jnp.einsum('bjhd,bjhv->bhdv', ki * kdec[None, :, :, None], vi)
        return R, o

    r0 = jnp.zeros((bsz, RET_HEADS, RET_DK, RET_DV), F32)
    _, o = lax.scan(step, r0, (to_chunks(q), to_chunks(k), to_chunks(v)))
    return from_chunks(o)


def hybrid_layer(x, pre_norm, post_norm, w_in, gdn_conv, gdn_A_log, gdn_dt_bias, gdn_norm,
                 ssd_conv, ssd_conv_b, ssd_A_log, ssd_dt_bias, ssd_D, ssd_norm, ret_norm, w_out, pos):
    bsz, seq, _ = x.shape
    h = rmsnorm(x, pre_norm)
    proj = jnp.einsum('bld,de->ble', h, w_in.astype(F32))
    (gq, gk, gv, gz, gb, ga, sx, sB, sC, sz, sdt, rq, rk, rv, rg) = split_cols(proj, IN_SIZES)

    qkv = jax.nn.silu(causal_conv(jnp.concatenate([gq, gk, gv], axis=-1), gdn_conv))
    gq, gk, gv = split_cols(qkv, [GDN_HEADS * GDN_DK, GDN_HEADS * GDN_DK, GDN_W])
    o_a = gated_deltanet(gq.reshape(bsz, seq, GDN_HEADS, GDN_DK), gk.reshape(bsz, seq, GDN_HEADS, GDN_DK),
                         gv.reshape(bsz, seq, GDN_HEADS, GDN_DV), gb, ga, gdn_A_log, gdn_dt_bias)
    o_a = rmsnorm(o_a, gdn_norm) * jax.nn.silu(gz.reshape(bsz, seq, GDN_HEADS, GDN_DV))

    xbc = jax.nn.silu(causal_conv(jnp.concatenate([sx, sB, sC], axis=-1), ssd_conv) + ssd_conv_b.astype(F32))
    sx, sB, sC = split_cols(xbc, [SSD_W, SSD_GROUPS * SSD_N, SSD_GROUPS * SSD_N])
    y_b = ssd(sx.reshape(bsz, seq, SSD_HEADS, SSD_P), sB.reshape(bsz, seq, SSD_GROUPS, SSD_N),
              sC.reshape(bsz, seq, SSD_GROUPS, SSD_N), sdt, ssd_A_log, ssd_dt_bias, ssd_D)
    y_b = y_b * jax.nn.silu(sz.reshape(bsz, seq, SSD_HEADS, SSD_P))
    o_b = rmsnorm(y_b.reshape(bsz, seq, SSD_GROUPS, SSD_W // SSD_GROUPS),
                  ssd_norm.reshape(SSD_GROUPS, SSD_W // SSD_GROUPS))

    q_c = rotary(rq.reshape(bsz, seq, RET_HEADS, RET_DK), pos)
    k_c = rotary(rk.reshape(bsz, seq, RET_HEADS, RET_DK), pos)
    o_c = retention(q_c, k_c, rv.reshape(bsz, seq, RET_HEADS, RET_DV))
    o_c = rmsnorm(o_c, ret_norm) * jax.nn.silu(rg.reshape(bsz, seq, RET_HEADS, RET_DV))

    mixed = jnp.concatenate([o_a.reshape(bsz, seq, GDN_W), o_b.reshape(bsz, seq, SSD_W),
                             o_c.reshape(bsz, seq, RET_W)], axis=-1)
    out = jnp.einsum('ble,ed->bld', mixed, w_out.astype(F32))
    return x + rmsnorm(out, post_norm).astype(x.dtype)


def setup_inputs(seed: int = 0) -> dict:
    key = jax.random.key(seed)
    ks = jax.random.split(key, 16)
    nrm = jax.random.normal

    def inv_softplus_dt(k, shape):
        dt = jnp.exp(jax.random.uniform(k, shape, F32, np.log(1e-3), np.log(1e-1)))
        return dt + jnp.log(-jnp.expm1(-dt))

    return {
        "x": nrm(ks[0], (BATCH, SEQ, D_MODEL), F32),
        "pre_norm": 1.0 + 0.05 * nrm(ks[1], (DEPTH, D_MODEL), F32),
        "post_norm": 1.0 + 0.05 * nrm(ks[2], (DEPTH, D_MODEL), F32),
        "w_in": nrm(ks[3], (DEPTH, D_MODEL, N_IN), F32) * D_MODEL ** -0.5,
        "gdn_conv": nrm(ks[4], (DEPTH, CONV_W, GDN_CONV_CH), F32) * CONV_W ** -0.5,
        "gdn_A_log": jnp.log(jax.random.uniform(ks[5], (DEPTH, GDN_HEADS), F32, 1.0, 16.0)),
        "gdn_dt_bias": inv_softplus_dt(ks[6], (DEPTH, GDN_HEADS)),
        "gdn_norm": 1.0 + 0.05 * nrm(ks[7], (DEPTH, GDN_DV), F32),
        "ssd_conv": nrm(ks[8], (DEPTH, CONV_W, SSD_CONV_CH), F32) * CONV_W ** -0.5,
        "ssd_conv_b": 0.02 * nrm(ks[9], (DEPTH, SSD_CONV_CH), F32),
        "ssd_A_log": jnp.log(jax.random.uniform(ks[10], (DEPTH, SSD_HEADS), F32, 1.0, 16.0)),
        "ssd_dt_bias": inv_softplus_dt(ks[11], (DEPTH, SSD_HEADS)),
        "ssd_D": 1.0 + 0.1 * nrm(ks[12], (DEPTH, SSD_HEADS), F32),
        "ssd_norm": 1.0 + 0.05 * nrm(ks[13], (DEPTH, SSD_W), F32),
        "ret_norm": 1.0 + 0.05 * nrm(ks[14], (DEPTH, RET_DV), F32),
        "w_out": nrm(ks[15], (DEPTH, MIX_W, D_MODEL), F32) * MIX_W ** -0.5,
    }


def reference(x, pre_norm, post_norm, w_in, gdn_conv, gdn_A_log, gdn_dt_bias, gdn_norm,
              ssd_conv, ssd_conv_b, ssd_A_log, ssd_dt_bias, ssd_D, ssd_norm, ret_norm, w_out):
    pos = jnp.arange(x.shape[1], dtype=jnp.int32)
    for l in range(DEPTH):
        x = hybrid_layer(x, pre_norm[l], post_norm[l], w_in[l], gdn_conv[l], gdn_A_log[l], gdn_dt_bias[l],
                         gdn_norm[l], ssd_conv[l], ssd_conv_b[l], ssd_A_log[l], ssd_dt_bias[l], ssd_D[l],
                         ssd_norm[l], ret_norm[l], w_out[l], pos)
    return x
```

```python
import functools

import jax
import jax.numpy as jnp
from jax import lax
from jax.experimental import pallas as pl
from jax.experimental.pallas import tpu as pltpu

F32 = jnp.float32
BF16 = jnp.bfloat16

D_MODEL = 1024
CHUNK = 64
CONV_W = 4
CONV_PAD = 8
EPS = 1e-6
LANES = 128

GDN_HEADS, GDN_D = 4, 128
GDN_W = GDN_HEADS * GDN_D
SSD_HEADS, SSD_P, SSD_N, SSD_GROUPS = 16, 64, 128, 2
SSD_W = SSD_HEADS * SSD_P
SSD_GW = SSD_W // SSD_GROUPS
RET_HEADS, RET_D = 4, 128
RET_W = RET_HEADS * RET_D
ROPE_BASE = 10000.0
MIX_W = GDN_W + SSD_W + RET_W

GDN_SIZES = [GDN_W, GDN_W, GDN_W, GDN_W, GDN_HEADS, GDN_HEADS]
SSD_SIZES = [SSD_W, SSD_GROUPS * SSD_N, SSD_GROUPS * SSD_N, SSD_W, SSD_HEADS]
RET_SIZES = [RET_W, RET_W, RET_W, RET_W]
IN_SIZES = GDN_SIZES + SSD_SIZES + RET_SIZES

GDN_CONV_CH = 3 * GDN_W
SSD_CONV_CH = SSD_W + 2 * SSD_GROUPS * SSD_N
GDN_PW = 4 * GDN_W + LANES
SSD_PW = SSD_CONV_CH + SSD_W + LANES
RET_PW = 4 * RET_W
GDN_ALPHA_LANE = GDN_HEADS

VMEM_LIMIT_BYTES = 48 * 1024 * 1024


def _mm(a, b):
    return jnp.dot(a.astype(BF16), b.astype(BF16), preferred_element_type=F32)


def _mm_nt(a, b):
    return lax.dot_general(a.astype(BF16), b.astype(BF16), (((1,), (1,)), ((), ())),
                           preferred_element_type=F32)


def _mm_tn(a, b):
    return lax.dot_general(a.astype(BF16), b.astype(BF16), (((0,), (0,)), ((), ())),
                           preferred_element_type=F32)


def _mm_f32(a, b):
    return jnp.dot(a, b, precision=lax.Precision.HIGHEST, preferred_element_type=F32)


def _sigmoid(x):
    return 1.0 / (1.0 + jnp.exp(-x))


def _silu(x):
    return x * _sigmoid(x)


def _softplus(x):
    return jnp.maximum(x, 0.0) + jnp.log(1.0 + jnp.exp(-jnp.abs(x)))


def _rms(x, w):
    return x * lax.rsqrt(jnp.mean(x * x, axis=-1, keepdims=True) + EPS) * w


def _project(x_ref, pn_ref, w_ref, p_ref, tb):
    h = _rms(x_ref[...], pn_ref[...]).astype(BF16)
    p_ref[CONV_PAD:CONV_PAD + tb, :] = jnp.dot(h, w_ref[...], preferred_element_type=F32)


def _carry_history(p_ref, tb, first):
    @pl.when(first)
    def _():
        p_ref[0:CONV_PAD, :] = jnp.zeros((CONV_PAD, p_ref.shape[1]), F32)

    @pl.when(jnp.logical_not(first))
    def _():
        p_ref[0:CONV_PAD, :] = p_ref[tb:tb + CONV_PAD, :]


def _conv_chunk(p_ref, base, width, cw):
    win = p_ref[pl.ds(base, CHUNK + CONV_PAD), 0:width]
    acc = win[CONV_PAD:CONV_PAD + CHUNK] * cw[CONV_W - 1:CONV_W]
    for i in range(CONV_W - 1):
        lo = CONV_PAD - (CONV_W - 1) + i
        acc = acc + win[lo:lo + CHUNK] * cw[i:i + 1]
    return acc


def _tri_masks():
    row = lax.broadcasted_iota(jnp.int32, (CHUNK, CHUNK), 0)
    col = lax.broadcasted_iota(jnp.int32, (CHUNK, CHUNK), 1)
    return row >= col, row > col, row == col


def _inv_unit_lower(a, eye):
    t = eye - a
    p = a
    k = 2
    while k < CHUNK:
        p = _mm_f32(p, p)
        t = t + _mm_f32(t, p)
        k *= 2
    return t


def _gdn_kernel(x_ref, pn_ref, w_ref, cw_ref, alog_ref, dtb_ref, gn_ref, o_ref, p_ref, s_ref, *, tb):
    first = pl.program_id(1) == 0
    _carry_history(p_ref, tb, first)

    @pl.when(first)
    def _():
        s_ref[...] = jnp.zeros_like(s_ref)

    _project(x_ref, pn_ref, w_ref, p_ref, tb)

    causal, strict, diag = _tri_masks()
    ltri = causal.astype(F32)
    eye = diag.astype(F32)
    cw = cw_ref[...]
    neg_a = -jnp.exp(alog_ref[...])
    dtb = dtb_ref[...]
    gn = gn_ref[...]

    def chunk(c, carry):
        base = pl.multiple_of(c * CHUNK, CHUNK)
        rows = pl.ds(base + CONV_PAD, CHUNK)
        qkv = _silu(_conv_chunk(p_ref, base, GDN_CONV_CH, cw))
        z = p_ref[rows, GDN_CONV_CH:GDN_CONV_CH + GDN_W]
        gt = p_ref[rows, 4 * GDN_W:GDN_PW]
        beta = _sigmoid(gt)
        g = neg_a * _softplus(gt + dtb)
        gcum = _mm_f32(ltri, g)
        gcum_t = gcum.T
        egc = jnp.exp(gcum)
        for h in range(GDN_HEADS):
            al = GDN_ALPHA_LANE + h
            q = qkv[:, h * GDN_D:(h + 1) * GDN_D]
            k = qkv[:, GDN_W + h * GDN_D:GDN_W + (h + 1) * GDN_D]
            v = qkv[:, 2 * GDN_W + h * GDN_D:2 * GDN_W + (h + 1) * GDN_D]
            q = q * lax.rsqrt(jnp.sum(q * q, axis=-1, keepdims=True) + EPS) * (GDN_D ** -0.5)
            k = k * lax.rsqrt(jnp.sum(k * k, axis=-1, keepdims=True) + EPS)
            b = beta[:, h:h + 1]
            gc = gcum[:, al:al + 1]
            diff = gc - gcum_t[al:al + 1, :]
            dec = jnp.exp(jnp.where(causal, diff, -jnp.inf))
            kb = k * b
            a_low = jnp.where(strict, _mm_nt(kb, k) * dec, 0.0)
            t_inv = _inv_unit_lower(a_low, eye)
            eg = egc[:, al:al + 1]
            sol = _mm(t_inv, jnp.concatenate([v * b, kb * eg], axis=1))
            u, w = sol[:, :GDN_D], sol[:, GDN_D:]
            s = s_ref[h]
            v_new = u - _mm(w, s)
            attn = _mm_nt(q, k) * dec
            o = _mm(q * eg, s) + _mm(attn, v_new)
            gl = gcum[CHUNK - 1:CHUNK, al:al + 1]
            s_ref[h] = s * jnp.exp(gl) + _mm_tn(k * jnp.exp(gl - gc), v_new)
            zh = z[:, h * GDN_D:(h + 1) * GDN_D]
            o_ref[pl.ds(base, CHUNK), h * GDN_D:(h + 1) * GDN_D] = _rms(o, gn) * _silu(zh)
        return carry

    lax.fori_loop(0, tb // CHUNK, chunk, 0)


def _ssd_kernel(x_ref, pn_ref, w_ref, cw_ref, cb_ref, alog_ref, dtb_ref, dx_ref, nrm_ref,
                o_ref, p_ref, hs_ref, *, tb):
    first = pl.program_id(1) == 0
    _carry_history(p_ref, tb, first)

    @pl.when(first)
    def _():
        hs_ref[...] = jnp.zeros_like(hs_ref)

    _project(x_ref, pn_ref, w_ref, p_ref, tb)

    causal, _, _ = _tri_masks()
    ltri = causal.astype(F32)
    expand = (lax.broadcasted_iota(jnp.int32, (LANES, SSD_W), 1) // SSD_P
              == lax.broadcasted_iota(jnp.int32, (LANES, SSD_W), 0)).astype(F32)
    cw = cw_ref[...]
    cbias = cb_ref[...]
    neg_a = -jnp.exp(alog_ref[...])
    dtb = dtb_ref[...]

    def chunk(c, carry):
        base = pl.multiple_of(c * CHUNK, CHUNK)
        rows = pl.ds(base + CONV_PAD, CHUNK)
        xbc = _silu(_conv_chunk(p_ref, base, SSD_CONV_CH, cw) + cbias)
        xs = xbc[:, :SSD_W]
        z = p_ref[rows, SSD_CONV_CH:SSD_CONV_CH + SSD_W]
        dt = _softplus(p_ref[rows, SSD_CONV_CH + SSD_W:SSD_PW] + dtb)
        acum = _mm_f32(ltri, dt * neg_a)
        acum_t = acum.T
        dt_x = _mm_f32(dt, expand)
        ac_x = _mm_f32(acum, expand)
        xdt = xs * dt_x
        eac_x = jnp.exp(ac_x)
        al_x = ac_x[CHUNK - 1:CHUNK, :]
        wx = jnp.exp(al_x - ac_x) * xdt
        for g in range(SSD_GROUPS):
            gs = slice(g * SSD_GW, (g + 1) * SSD_GW)
            bg = xbc[:, SSD_W + g * SSD_N:SSD_W + (g + 1) * SSD_N]
            cg = xbc[:, SSD_W + SSD_GROUPS * SSD_N + g * SSD_N:SSD_W + SSD_GROUPS * SSD_N + (g + 1) * SSD_N]
            cbm = _mm_nt(cg, bg)
            hs = hs_ref[g]
            y_inter = _mm(cg, hs) * eac_x[:, gs]
            ys = []
            for hh in range(SSD_HEADS // SSD_GROUPS):
                h = g * (SSD_HEADS // SSD_GROUPS) + hh
                diff = acum[:, h:h + 1] - acum_t[h:h + 1, :]
                lmat = jnp.exp(jnp.where(causal, diff, -jnp.inf))
                ys.append(_mm(cbm * lmat, xdt[:, h * SSD_P:(h + 1) * SSD_P]))
            y = jnp.concatenate(ys, axis=1) + y_inter + xs[:, gs] * dx_ref[:, gs]
            hs_ref[g] = hs * jnp.exp(al_x[:, gs]) + _mm_tn(bg, wx[:, gs])
            y = y * _silu(z[:, gs])
            o_ref[pl.ds(base, CHUNK), gs] = _rms(y, nrm_ref[:, gs])
        return carry

    lax.fori_loop(0, tb // CHUNK, chunk, 0)


def _ret_kernel(x_ref, pn_ref, w_ref, cos_ref, sin_ref, dmat_ref, qdec_ref, kdec_ref, cdec_ref, nrm_ref,
                o_ref, p_ref, r_ref, *, tb):
    first = pl.program_id(1) == 0

    @pl.when(first)
    def _():
        r_ref[...] = jnp.zeros_like(r_ref)

    _project(x_ref, pn_ref, w_ref, p_ref, tb)
    nrm = nrm_ref[...]

    def chunk(c, carry):
        base = pl.multiple_of(c * CHUNK, CHUNK)
        rows = pl.ds(base + CONV_PAD, CHUNK)
        cosf = cos_ref[pl.ds(base, CHUNK), :]
        sinf = sin_ref[pl.ds(base, CHUNK), :]
        for h in range(RET_HEADS):
            hs = slice(h * RET_D, (h + 1) * RET_D)
            q = p_ref[rows, h * RET_D:(h + 1) * RET_D]
            k = p_ref[rows, RET_W + h * RET_D:RET_W + (h + 1) * RET_D]
            v = p_ref[rows, 2 * RET_W + h * RET_D:2 * RET_W + (h + 1) * RET_D]
            gate = p_ref[rows, 3 * RET_W + h * RET_D:3 * RET_W + (h + 1) * RET_D]
            q = q * cosf + pltpu.roll(q, RET_D // 2, 1) * sinf
            k = (k * cosf + pltpu.roll(k, RET_D // 2, 1) * sinf) * (RET_D ** -0.5)
            s = _mm_nt(q, k) * dmat_ref[h]
            r = r_ref[h]
            o = _mm(s, v) + _mm(q, r) * qdec_ref[:, hs]
            r_ref[h] = r * cdec_ref[:, hs] + _mm_tn(k * kdec_ref[:, hs], v)
            o_ref[pl.ds(base, CHUNK), hs] = _rms(o, nrm) * _silu(gate)
        return carry

    lax.fori_loop(0, tb // CHUNK, chunk, 0)


def _out_kernel(x_ref, a_ref, b_ref, c_ref, wa_ref, wb_ref, wc_ref, pn_ref, o_ref):
    out = (jnp.dot(a_ref[...].astype(BF16), wa_ref[...], preferred_element_type=F32)
           + jnp.dot(b_ref[...].astype(BF16), wb_ref[...], preferred_element_type=F32)
           + jnp.dot(c_ref[...].astype(BF16), wc_ref[...], preferred_element_type=F32))
    o_ref[...] = x_ref[...] + _rms(out, pn_ref[...])


def _pad_lanes(t):
    return jnp.pad(t, ((0, 0), (0, LANES - t.shape[-1])))


def _split(t, sizes):
    out, start = [], 0
    for s in sizes:
        out.append(t[..., start:start + s])
        start += s
    return out


def _const_spec(shape):
    return pl.BlockSpec(shape, lambda b, t: (0,) * len(shape))


def _mixer_call(body, x, consts, const_shapes, pw, out_w, state_shape, tb, seq_consts=()):
    bsz, seq, _ = x.shape
    in_specs = [pl.BlockSpec((None, tb, D_MODEL), lambda b, t: (b, t, 0))]
    in_specs += [_const_spec(s) for s in const_shapes]
    return pl.pallas_call(
        functools.partial(body, tb=tb),
        grid=(bsz, seq // tb),
        in_specs=in_specs,
        out_specs=pl.BlockSpec((None, tb, out_w), lambda b, t: (b, t, 0)),
        out_shape=jax.ShapeDtypeStruct((bsz, seq, out_w), F32),
        scratch_shapes=[pltpu.VMEM((tb + CONV_PAD, pw), F32), pltpu.VMEM(state_shape, F32)],
        compiler_params=pltpu.CompilerParams(dimension_semantics=("arbitrary", "arbitrary"),
                                             vmem_limit_bytes=VMEM_LIMIT_BYTES),
    )(x, *consts)


def _retention_tables(seq):
    half = RET_D // 2
    pos = jnp.arange(seq, dtype=jnp.int32)
    inv = ROPE_BASE ** (-jnp.arange(half, dtype=F32) / half)
    ang = pos.astype(F32)[:, None] * inv[None, :]
    cos, sin = jnp.cos(ang), jnp.sin(ang)
    cosf = jnp.concatenate([cos, cos], axis=-1)
    sinf = jnp.concatenate([-sin, sin], axis=-1)
    lg = jnp.log(1.0 - 2.0 ** (-5.0 - jnp.arange(RET_HEADS, dtype=F32)))
    idx = jnp.arange(CHUNK, dtype=F32)
    rel = idx[:, None] - idx[None, :]
    dmat = jnp.where(rel[None] >= 0, jnp.exp(jnp.maximum(rel, 0.0)[None] * lg[:, None, None]), 0.0)
    qdec = jnp.exp((idx[:, None] + 1.0) * lg[None, :])
    kdec = jnp.exp((CHUNK - 1.0 - idx)[:, None] * lg[None, :])
    cdec = jnp.exp(CHUNK * lg)
    rep = lambda t: jnp.repeat(t, RET_D, axis=-1)
    return cosf, sinf, dmat, rep(qdec), rep(kdec), rep(cdec[None, :])


def _layer(x, pre_norm, post_norm, w_in, gdn_conv, gdn_A_log, gdn_dt_bias, gdn_norm, ssd_conv, ssd_conv_b,
           ssd_A_log, ssd_dt_bias, ssd_D, ssd_norm, ret_norm, w_out, tables, tb, tm):
    bsz, seq, _ = x.shape
    (gq, gk, gv, gz, gb, ga, sx, sb, sc, sz, sdt, rq, rk, rv, rg) = _split(w_in, IN_SIZES)
    w_g = jnp.concatenate([gq, gk, gv, gz, _pad_lanes(jnp.concatenate([gb, ga], axis=-1))], axis=-1).astype(BF16)
    w_s = jnp.concatenate([sx, sb, sc, sz, _pad_lanes(sdt)], axis=-1).astype(BF16)
    w_r = jnp.concatenate([rq, rk, rv, rg], axis=-1).astype(BF16)
    pn = pre_norm[None, :]

    g_alog = _pad_lanes(jnp.concatenate([jnp.zeros((GDN_HEADS,), F32), gdn_A_log])[None, :])
    g_dtb = _pad_lanes(jnp.concatenate([jnp.zeros((GDN_HEADS,), F32), gdn_dt_bias])[None, :])
    o_a = _mixer_call(
        _gdn_kernel, x,
        [pn, w_g, gdn_conv, g_alog, g_dtb, gdn_norm[None, :]],
        [(1, D_MODEL), (D_MODEL, GDN_PW), (CONV_W, GDN_CONV_CH), (1, LANES), (1, LANES), (1, GDN_D)],
        GDN_PW, GDN_W, (GDN_HEADS, GDN_D, GDN_D), tb)

    o_b = _mixer_call(
        _ssd_kernel, x,
        [pn, w_s, ssd_conv, ssd_conv_b[None, :], _pad_lanes(ssd_A_log[None, :]), _pad_lanes(ssd_dt_bias[None, :]),
         jnp.repeat(ssd_D, SSD_P)[None, :], ssd_norm[None, :]],
        [(1, D_MODEL), (D_MODEL, SSD_PW), (CONV_W, SSD_CONV_CH), (1, SSD_CONV_CH), (1, LANES), (1, LANES),
         (1, SSD_W), (1, SSD_W)],
        SSD_PW, SSD_W, (SSD_GROUPS, SSD_N, SSD_GW), tb)

    cosf, sinf, dmat, qdec, kdec, cdec = tables
    o_c = pl.pallas_call(
        functools.partial(_ret_kernel, tb=tb),
        grid=(bsz, seq // tb),
        in_specs=[pl.BlockSpec((None, tb, D_MODEL), lambda b, t: (b, t, 0)),
                  _const_spec((1, D_MODEL)), _const_spec((D_MODEL, RET_PW)),
                  pl.BlockSpec((tb, RET_D), lambda b, t: (t, 0)),
                  pl.BlockSpec((tb, RET_D), lambda b, t: (t, 0)),
                  _const_spec((RET_HEADS, CHUNK, CHUNK)), _const_spec((CHUNK, RET_W)),
                  _const_spec((CHUNK, RET_W)), _const_spec((1, RET_W)), _const_spec((1, RET_D))],
        out_specs=pl.BlockSpec((None, tb, RET_W), lambda b, t: (b, t, 0)),
        out_shape=jax.ShapeDtypeStruct((bsz, seq, RET_W), F32),
        scratch_shapes=[pltpu.VMEM((tb + CONV_PAD, RET_PW), F32), pltpu.VMEM((RET_HEADS, RET_D, RET_D), F32)],
        compiler_params=pltpu.CompilerParams(dimension_semantics=("arbitrary", "arbitrary"),
                                             vmem_limit_bytes=VMEM_LIMIT_BYTES),
    )(x, pn, w_r, cosf, sinf, dmat, qdec, kdec, cdec, ret_norm[None, :])

    n_tok = bsz * seq
    w_o = w_out.astype(BF16)
    tok = lambda w: pl.BlockSpec((tm, w), lambda i: (i, 0))
    full = lambda r, w: pl.BlockSpec((r, w), lambda i: (0, 0))
    y = pl.pallas_call(
        _out_kernel,
        grid=(n_tok // tm,),
        in_specs=[tok(D_MODEL), tok(GDN_W), tok(SSD_W), tok(RET_W),
                  full(GDN_W, D_MODEL), full(SSD_W, D_MODEL), full(RET_W, D_MODEL), full(1, D_MODEL)],
        out_specs=tok(D_MODEL),
        out_shape=jax.ShapeDtypeStruct((n_tok, D_MODEL), F32),
        compiler_params=pltpu.CompilerParams(dimension_semantics=("arbitrary",),
                                             vmem_limit_bytes=VMEM_LIMIT_BYTES),
    )(x.reshape(n_tok, D_MODEL), o_a.reshape(n_tok, GDN_W), o_b.reshape(n_tok, SSD_W),
      o_c.reshape(n_tok, RET_W), w_o[:GDN_W], w_o[GDN_W:GDN_W + SSD_W], w_o[GDN_W + SSD_W:], post_norm[None, :])
    return y.reshape(bsz, seq, D_MODEL)


def kernel(x, pre_norm, post_norm, w_in, gdn_conv, gdn_A_log, gdn_dt_bias, gdn_norm, ssd_conv, ssd_conv_b,
           ssd_A_log, ssd_dt_bias, ssd_D, ssd_norm, ret_norm, w_out):
    seq = x.shape[1]
    tb = min(512, seq)
    tm = min(512, x.shape[0] * seq)
    tables = _retention_tables(seq)
    for l in range(pre_norm.shape[0]):
        x = _layer(x, pre_norm[l], post_norm[l], w_in[l], gdn_conv[l], gdn_A_log[l], gdn_dt_bias[l], gdn_norm[l],
                   ssd_conv[l], ssd_conv_b[l], ssd_A_log[l], ssd_dt_bias[l], ssd_D[l], ssd_norm[l], ret_norm[l],
                   w_out[l], tables, tb, tm)
    return x
```

```python
import functools

import jax
import jax.numpy as jnp
from jax import lax
from jax.experimental import pallas as pl
from jax.experimental.pallas import tpu as pltpu

F32 = jnp.float32
BF16 = jnp.bfloat16

D_MODEL = 1024
CHUNK = 64
CONV_W = 4
CONV_PAD = 8
STEP_CHUNKS = 2
EPS = 1e-6
LANES = 128

GDN_HEADS, GDN_D = 4, 128
GDN_W = GDN_HEADS * GDN_D
SSD_HEADS, SSD_P, SSD_N, SSD_GROUPS = 16, 64, 128, 2
SSD_W = SSD_HEADS * SSD_P
SSD_GW = SSD_W // SSD_GROUPS
RET_HEADS, RET_D = 4, 128
RET_W = RET_HEADS * RET_D
ROPE_BASE = 10000.0
MIX_W = GDN_W + SSD_W + RET_W

GDN_SIZES = [GDN_W, GDN_W, GDN_W, GDN_W, GDN_HEADS, GDN_HEADS]
SSD_SIZES = [SSD_W, SSD_GROUPS * SSD_N, SSD_GROUPS * SSD_N, SSD_W, SSD_HEADS]
RET_SIZES = [RET_W, RET_W, RET_W, RET_W]
IN_SIZES = GDN_SIZES + SSD_SIZES + RET_SIZES

GDN_CONV_CH = 3 * GDN_W
SSD_CONV_CH = SSD_W + 2 * SSD_GROUPS * SSD_N
GDN_PW = 4 * GDN_W + LANES
SSD_PW = SSD_CONV_CH + SSD_W + LANES
RET_PW = 4 * RET_W
GDN_ALPHA_LANE = GDN_HEADS

VMEM_LIMIT_BYTES = 48 * 1024 * 1024


def _mm(a, b):
    return jnp.dot(a.astype(BF16), b.astype(BF16), preferred_element_type=F32)


def _mm_nt(a, b):
    return lax.dot_general(a.astype(BF16), b.astype(BF16), (((1,), (1,)), ((), ())),
                           preferred_element_type=F32)


def _mm_tn(a, b):
    return lax.dot_general(a.astype(BF16), b.astype(BF16), (((0,), (0,)), ((), ())),
                           preferred_element_type=F32)


def _split3(x):
    hi = x.astype(BF16)
    r = x - hi.astype(F32)
    mid = r.astype(BF16)
    lo = (r - mid.astype(F32)).astype(BF16)
    return hi, mid, lo


def _mm_sel(sel, x, pieces=3):
    sel = sel.astype(BF16)
    acc = None
    for piece in _split3(x)[:pieces]:
        d = jnp.dot(sel, piece, preferred_element_type=F32)
        acc = d if acc is None else acc + d
    return acc


def _mm_sel_r(x, sel, pieces=3):
    sel = sel.astype(BF16)
    acc = None
    for piece in _split3(x)[:pieces]:
        d = jnp.dot(piece, sel, preferred_element_type=F32)
        acc = d if acc is None else acc + d
    return acc


def _sigmoid(x):
    return 1.0 / (1.0 + jnp.exp(-x))


def _silu(x):
    return x * _sigmoid(x)


def _softplus(x):
    return jnp.maximum(x, 0.0) + jnp.log(1.0 + jnp.exp(-jnp.abs(x)))


def _rms(x, w):
    return x * lax.rsqrt(jnp.mean(x * x, axis=-1, keepdims=True) + EPS) * w


def _project(x_ref, pn_ref, w_ref, p_ref, tb):
    h = _rms(x_ref[...], pn_ref[...]).astype(BF16)
    p_ref[CONV_PAD:CONV_PAD + tb, :] = jnp.dot(h, w_ref[...], preferred_element_type=F32)


def _carry_history(p_ref, tb, first):
    @pl.when(first)
    def _():
        p_ref[0:CONV_PAD, :] = jnp.zeros((CONV_PAD, p_ref.shape[1]), F32)

    @pl.when(jnp.logical_not(first))
    def _():
        p_ref[0:CONV_PAD, :] = p_ref[tb:tb + CONV_PAD, :]


def _conv_chunk(p_ref, base, width, cw):
    win = p_ref[pl.ds(base, CHUNK + CONV_PAD), 0:width]
    acc = win[CONV_PAD:CONV_PAD + CHUNK] * cw[CONV_W - 1:CONV_W]
    for i in range(CONV_W - 1):
        lo = CONV_PAD - (CONV_W - 1) + i
        acc = acc + win[lo:lo + CHUNK] * cw[i:i + 1]
    return acc


def _tri_masks():
    row = lax.broadcasted_iota(jnp.int32, (CHUNK, CHUNK), 0)
    col = lax.broadcasted_iota(jnp.int32, (CHUNK, CHUNK), 1)
    return row >= col, row > col


def _solve_unit_lower(a_list, rhs_list):
    lane = lax.broadcasted_iota(jnp.int32, (CHUNK, 2 * CHUNK), 1)
    row = lax.broadcasted_iota(jnp.int32, (CHUNK, 2 * CHUNK), 0)
    left = lane < CHUNK
    zeros = jnp.zeros((CHUNK, 2 * CHUNK), F32)
    pts = [jnp.where(lane == row + CHUNK, 1.0, jnp.concatenate([-a, jnp.zeros_like(a)], axis=1)) for a in a_list]
    k = 1
    while k < CHUNK:
        news = [_mm(jnp.where(left, pt, 0.0), jnp.concatenate([pt, zeros], axis=0)) for pt in pts]
        pts = [new + jnp.where(left, 0.0, pt) for new, pt in zip(news, pts)]
        k *= 2
    return [_mm(jnp.where(left, 0.0, pt), jnp.concatenate([jnp.zeros_like(rhs), rhs], axis=0))
            for pt, rhs in zip(pts, rhs_list)]


def _gdn_kernel(x_ref, pn_ref, w_ref, cw_ref, alog_ref, dtb_ref, gn_ref, o_ref, p_ref, s_ref, *, tb):
    first = pl.program_id(1) == 0
    _carry_history(p_ref, tb, first)

    @pl.when(first)
    def _():
        s_ref[...] = jnp.zeros_like(s_ref)

    _project(x_ref, pn_ref, w_ref, p_ref, tb)

    causal, strict = _tri_masks()
    ltri = causal.astype(F32)
    cw = cw_ref[...]
    neg_a = -jnp.exp(alog_ref[...])
    dtb = dtb_ref[...]
    gn = gn_ref[...]
    heads = range(GDN_HEADS)

    def step(i, carry):
        probs = []
        for u in range(STEP_CHUNKS):
            base = pl.multiple_of((i * STEP_CHUNKS + u) * CHUNK, CHUNK)
            rows = pl.ds(base + CONV_PAD, CHUNK)
            qkv = _silu(_conv_chunk(p_ref, base, GDN_CONV_CH, cw))
            z = p_ref[rows, GDN_CONV_CH:GDN_CONV_CH + GDN_W]
            gt = p_ref[rows, 4 * GDN_W:GDN_PW]
            beta = _sigmoid(gt)
            g = neg_a * _softplus(gt + dtb)
            gcum = _mm_sel(ltri, g)
            gcum_t = gcum.T
            egc = jnp.exp(gcum)
            for h in heads:
                al = GDN_ALPHA_LANE + h
                q = qkv[:, h * GDN_D:(h + 1) * GDN_D]
                k = qkv[:, GDN_W + h * GDN_D:GDN_W + (h + 1) * GDN_D]
                v = qkv[:, 2 * GDN_W + h * GDN_D:2 * GDN_W + (h + 1) * GDN_D]
                q = q * lax.rsqrt(jnp.sum(q * q, axis=-1, keepdims=True) + EPS) * (GDN_D ** -0.5)
                k = k * lax.rsqrt(jnp.sum(k * k, axis=-1, keepdims=True) + EPS)
                b = beta[:, h:h + 1]
                gc = gcum[:, al:al + 1]
                eg = egc[:, al:al + 1]
                gl = gcum[CHUNK - 1:CHUNK, al:al + 1]
                kb = k * b
                probs.append(dict(
                    base=base, h=h, q=q, k=k, kb=kb, qe=q * eg,
                    dec=jnp.exp(jnp.where(causal, gc - gcum_t[al:al + 1, :], -jnp.inf)),
                    rhs=jnp.concatenate([v * b, kb * eg], axis=1),
                    kdec=k * jnp.exp(gl - gc), egl=jnp.exp(gl),
                    zh=z[:, h * GDN_D:(h + 1) * GDN_D]))
        kk = [_mm_nt(p["kb"], p["k"]) for p in probs]
        qk = [_mm_nt(p["q"], p["k"]) for p in probs]
        sols = _solve_unit_lower([jnp.where(strict, m * p["dec"], 0.0) for m, p in zip(kk, probs)],
                                 [p["rhs"] for p in probs])
        attn = [m * p["dec"] for m, p in zip(qk, probs)]
        state = [s_ref[h] for h in heads]
        for u in range(STEP_CHUNKS):
            sl = slice(u * GDN_HEADS, (u + 1) * GDN_HEADS)
            ps, so, at = probs[sl], sols[sl], attn[sl]
            ws = [_mm(so[h][:, GDN_D:], state[h]) for h in heads]
            qs = [_mm(ps[h]["qe"], state[h]) for h in heads]
            v_new = [so[h][:, :GDN_D] - ws[h] for h in heads]
            av = [_mm(at[h], v_new[h]) for h in heads]
            kv = [_mm_tn(ps[h]["kdec"], v_new[h]) for h in heads]
            state = [state[h] * ps[h]["egl"] + kv[h] for h in heads]
            for h in heads:
                o_ref[pl.ds(ps[h]["base"], CHUNK), h * GDN_D:(h + 1) * GDN_D] = (
                    _rms(qs[h] + av[h], gn) * _silu(ps[h]["zh"]))
        for h in heads:
            s_ref[h] = state[h]
        return carry

    lax.fori_loop(0, tb // (CHUNK * STEP_CHUNKS), step, 0)


def _ssd_kernel(x_ref, pn_ref, w_ref, cw_ref, cb_ref, alog_ref, dtb_ref, dx_ref, nrm_ref,
                o_ref, p_ref, hs_ref, *, tb):
    first = pl.program_id(1) == 0
    _carry_history(p_ref, tb, first)

    @pl.when(first)
    def _():
        hs_ref[...] = jnp.zeros_like(hs_ref)

    _project(x_ref, pn_ref, w_ref, p_ref, tb)

    causal, _ = _tri_masks()
    ltri = causal.astype(F32)
    expand = (lax.broadcasted_iota(jnp.int32, (LANES, SSD_W), 1) // SSD_P
              == lax.broadcasted_iota(jnp.int32, (LANES, SSD_W), 0)).astype(F32)
    cw = cw_ref[...]
    cbias = cb_ref[...]
    neg_a = -jnp.exp(alog_ref[...])
    dtb = dtb_ref[...]
    groups = range(SSD_GROUPS)
    hpg = SSD_HEADS // SSD_GROUPS

    def step(i, carry):
        chunks = []
        for u in range(STEP_CHUNKS):
            base = pl.multiple_of((i * STEP_CHUNKS + u) * CHUNK, CHUNK)
            rows = pl.ds(base + CONV_PAD, CHUNK)
            xbc = _silu(_conv_chunk(p_ref, base, SSD_CONV_CH, cw) + cbias)
            dt = _softplus(p_ref[rows, SSD_CONV_CH + SSD_W:SSD_PW] + dtb)
            chunks.append(dict(
                base=base, xs=xbc[:, :SSD_W], dt=dt,
                bg=[xbc[:, SSD_W + g * SSD_N:SSD_W + (g + 1) * SSD_N] for g in groups],
                cg=[xbc[:, SSD_W + (SSD_GROUPS + g) * SSD_N:SSD_W + (SSD_GROUPS + g + 1) * SSD_N] for g in groups],
                z=p_ref[rows, SSD_CONV_CH:SSD_CONV_CH + SSD_W]))
        acums = [_mm_sel(ltri, c["dt"] * neg_a) for c in chunks]
        cbms = [[_mm_nt(c["cg"][g], c["bg"][g]) for g in groups] for c in chunks]
        dt_xs = [_mm_sel_r(c["dt"], expand, pieces=2) for c in chunks]
        ac_xs = [_mm_sel_r(a, expand, pieces=2) for a in acums]
        intra = []
        for c, acum, cbm, dt_x in zip(chunks, acums, cbms, dt_xs):
            acum_t = acum.T
            c["xdt"] = c["xs"] * dt_x
            ys = []
            for h in range(SSD_HEADS):
                lmat = jnp.exp(jnp.where(causal, acum[:, h:h + 1] - acum_t[h:h + 1, :], -jnp.inf))
                ys.append(_mm(cbm[h // hpg] * lmat, c["xdt"][:, h * SSD_P:(h + 1) * SSD_P]))
            intra.append(jnp.concatenate(ys, axis=1))
        state = [hs_ref[g] for g in groups]
        for c, ac_x, y_intra in zip(chunks, ac_xs, intra):
            eac_x = jnp.exp(ac_x)
            al_x = ac_x[CHUNK - 1:CHUNK, :]
            wx = jnp.exp(al_x - ac_x) * c["xdt"]
            y_inter = jnp.concatenate([_mm(c["cg"][g], state[g]) for g in groups], axis=1) * eac_x
            upd = [_mm_tn(c["bg"][g], wx[:, g * SSD_GW:(g + 1) * SSD_GW]) for g in groups]
            state = [state[g] * jnp.exp(al_x[:, g * SSD_GW:(g + 1) * SSD_GW]) + upd[g] for g in groups]
            y = (y_intra + y_inter + c["xs"] * dx_ref[...]) * _silu(c["z"])
            for g in groups:
                gs = slice(g * SSD_GW, (g + 1) * SSD_GW)
                o_ref[pl.ds(c["base"], CHUNK), gs] = _rms(y[:, gs], nrm_ref[:, gs])
        for g in groups:
            hs_ref[g] = state[g]
        return carry

    lax.fori_loop(0, tb // (CHUNK * STEP_CHUNKS), step, 0)


def _ret_kernel(x_ref, pn_ref, w_ref, cos_ref, sin_ref, dmat_ref, qdec_ref, kdec_ref, cdec_ref, nrm_ref,
                o_ref, p_ref, r_ref, *, tb):
    first = pl.program_id(1) == 0

    @pl.when(first)
    def _():
        r_ref[...] = jnp.zeros_like(r_ref)

    _project(x_ref, pn_ref, w_ref, p_ref, tb)
    nrm = nrm_ref[...]
    heads = range(RET_HEADS)

    def step(i, carry):
        probs = []
        for u in range(STEP_CHUNKS):
            base = pl.multiple_of((i * STEP_CHUNKS + u) * CHUNK, CHUNK)
            rows = pl.ds(base + CONV_PAD, CHUNK)
            cosf = cos_ref[pl.ds(base, CHUNK), :]
            sinf = sin_ref[pl.ds(base, CHUNK), :]
            for h in heads:
                hs = slice(h * RET_D, (h + 1) * RET_D)
                q = p_ref[rows, h * RET_D:(h + 1) * RET_D]
                k = p_ref[rows, RET_W + h * RET_D:RET_W + (h + 1) * RET_D]
                q = q * cosf + pltpu.roll(q, RET_D // 2, 1) * sinf
                k = (k * cosf + pltpu.roll(k, RET_D // 2, 1) * sinf) * (RET_D ** -0.5)
                probs.append(dict(
                    base=base, h=h, hs=hs, q=q, k=k, kd=k * kdec_ref[:, hs],
                    v=p_ref[rows, 2 * RET_W + h * RET_D:2 * RET_W + (h + 1) * RET_D],
                    gate=p_ref[rows, 3 * RET_W + h * RET_D:3 * RET_W + (h + 1) * RET_D]))
        scores = [_mm_nt(p["q"], p["k"]) * dmat_ref[p["h"]] for p in probs]
        kv = [_mm_tn(p["kd"], p["v"]) for p in probs]
        intra = [_mm(s, p["v"]) for s, p in zip(scores, probs)]
        state = [r_ref[h] for h in heads]
        for u in range(STEP_CHUNKS):
            sl = slice(u * RET_HEADS, (u + 1) * RET_HEADS)
            ps = probs[sl]
            qr = [_mm(ps[h]["q"], state[h]) for h in heads]
            state = [state[h] * cdec_ref[:, ps[h]["hs"]] + kv[sl][h] for h in heads]
            for h in heads:
                o = intra[sl][h] + qr[h] * qdec_ref[:, ps[h]["hs"]]
                o_ref[pl.ds(ps[h]["base"], CHUNK), ps[h]["hs"]] = _rms(o, nrm) * _silu(ps[h]["gate"])
        for h in heads:
            r_ref[h] = state[h]
        return carry

    lax.fori_loop(0, tb // (CHUNK * STEP_CHUNKS), step, 0)


def _out_kernel(x_ref, a_ref, b_ref, c_ref, wa_ref, wb_ref, wc_ref, pn_ref, o_ref):
    out = (jnp.dot(a_ref[...].astype(BF16), wa_ref[...], preferred_element_type=F32)
           + jnp.dot(b_ref[...].astype(BF16), wb_ref[...], preferred_element_type=F32)
           + jnp.dot(c_ref[...].astype(BF16), wc_ref[...], preferred_element_type=F32))
    o_ref[...] = x_ref[...] + _rms(out, pn_ref[...])


def _pad_lanes(t):
    return jnp.pad(t, ((0, 0), (0, LANES - t.shape[-1])))


def _split(t, sizes):
    out, start = [], 0
    for s in sizes:
        out.append(t[..., start:start + s])
        start += s
    return out


def _const_spec(shape):
    return pl.BlockSpec(shape, lambda b, t: (0,) * len(shape))


def _mixer_call(body, x, consts, const_shapes, pw, out_w, state_shape, tb):
    bsz, seq, _ = x.shape
    in_specs = [pl.BlockSpec((None, tb, D_MODEL), lambda b, t: (b, t, 0))]
    in_specs += [_const_spec(s) for s in const_shapes]
    return pl.pallas_call(
        functools.partial(body, tb=tb),
        grid=(bsz, seq // tb),
        in_specs=in_specs,
        out_specs=pl.BlockSpec((None, tb, out_w), lambda b, t: (b, t, 0)),
        out_shape=jax.ShapeDtypeStruct((bsz, seq, out_w), F32),
        scratch_shapes=[pltpu.VMEM((tb + CONV_PAD, pw), F32), pltpu.VMEM(state_shape, F32)],
        compiler_params=pltpu.CompilerParams(dimension_semantics=("arbitrary", "arbitrary"),
                                             vmem_limit_bytes=VMEM_LIMIT_BYTES),
    )(x, *consts)


def _retention_tables(seq):
    half = RET_D // 2
    pos = jnp.arange(seq, dtype=jnp.int32)
    inv = ROPE_BASE ** (-jnp.arange(half, dtype=F32) / half)
    ang = pos.astype(F32)[:, None] * inv[None, :]
    cos, sin = jnp.cos(ang), jnp.sin(ang)
    cosf = jnp.concatenate([cos, cos], axis=-1)
    sinf = jnp.concatenate([-sin, sin], axis=-1)
    lg = jnp.log(1.0 - 2.0 ** (-5.0 - jnp.arange(RET_HEADS, dtype=F32)))
    idx = jnp.arange(CHUNK, dtype=F32)
    rel = idx[:, None] - idx[None, :]
    dmat = jnp.where(rel[None] >= 0, jnp.exp(jnp.maximum(rel, 0.0)[None] * lg[:, None, None]), 0.0)
    qdec = jnp.exp((idx[:, None] + 1.0) * lg[None, :])
    kdec = jnp.exp((CHUNK - 1.0 - idx)[:, None] * lg[None, :])
    cdec = jnp.exp(CHUNK * lg)
    rep = lambda t: jnp.repeat(t, RET_D, axis=-1)
    return cosf, sinf, dmat, rep(qdec), rep(kdec), rep(cdec[None, :])


def _layer(x, pre_norm, post_norm, w_in, gdn_conv, gdn_A_log, gdn_dt_bias, gdn_norm, ssd_conv, ssd_conv_b,
           ssd_A_log, ssd_dt_bias, ssd_D, ssd_norm, ret_norm, w_out, tables, tb, tm):
    bsz, seq, _ = x.shape
    (gq, gk, gv, gz, gb, ga, sx, sb, sc, sz, sdt, rq, rk, rv, rg) = _split(w_in, IN_SIZES)
    w_g = jnp.concatenate([gq, gk, gv, gz, _pad_lanes(jnp.concatenate([gb, ga], axis=-1))], axis=-1).astype(BF16)
    w_s = jnp.concatenate([sx, sb, sc, sz, _pad_lanes(sdt)], axis=-1).astype(BF16)
    w_r = jnp.concatenate([rq, rk, rv, rg], axis=-1).astype(BF16)
    pn = pre_norm[None, :]

    g_alog = _pad_lanes(jnp.concatenate([jnp.zeros((GDN_HEADS,), F32), gdn_A_log])[None, :])
    g_dtb = _pad_lanes(jnp.concatenate([jnp.zeros((GDN_HEADS,), F32), gdn_dt_bias])[None, :])
    o_a = _mixer_call(
        _gdn_kernel, x,
        [pn, w_g, gdn_conv, g_alog, g_dtb, gdn_norm[None, :]],
        [(1, D_MODEL), (D_MODEL, GDN_PW), (CONV_W, GDN_CONV_CH), (1, LANES), (1, LANES), (1, GDN_D)],
        GDN_PW, GDN_W, (GDN_HEADS, GDN_D, GDN_D), tb)

    o_b = _mixer_call(
        _ssd_kernel, x,
        [pn, w_s, ssd_conv, ssd_conv_b[None, :], _pad_lanes(ssd_A_log[None, :]), _pad_lanes(ssd_dt_bias[None, :]),
         jnp.repeat(ssd_D, SSD_P)[None, :], ssd_norm[None, :]],
        [(1, D_MODEL), (D_MODEL, SSD_PW), (CONV_W, SSD_CONV_CH), (1, SSD_CONV_CH), (1, LANES), (1, LANES),
         (1, SSD_W), (1, SSD_W)],
        SSD_PW, SSD_W, (SSD_GROUPS, SSD_N, SSD_GW), tb)

    cosf, sinf, dmat, qdec, kdec, cdec = tables
    o_c = pl.pallas_call(
        functools.partial(_ret_kernel, tb=tb),
        grid=(bsz, seq // tb),
        in_specs=[pl.BlockSpec((None, tb, D_MODEL), lambda b, t: (b, t, 0)),
                  _const_spec((1, D_MODEL)), _const_spec((D_MODEL, RET_PW)),
                  pl.BlockSpec((tb, RET_D), lambda b, t: (t, 0)),
                  pl.BlockSpec((tb, RET_D), lambda b, t: (t, 0)),
                  _const_spec((RET_HEADS, CHUNK, CHUNK)), _const_spec((CHUNK, RET_W)),
                  _const_spec((CHUNK, RET_W)), _const_spec((1, RET_W)), _const_spec((1, RET_D))],
        out_specs=pl.BlockSpec((None, tb, RET_W), lambda b, t: (b, t, 0)),
        out_shape=jax.ShapeDtypeStruct((bsz, seq, RET_W), F32),
        scratch_shapes=[pltpu.VMEM((tb + CONV_PAD, RET_PW), F32), pltpu.VMEM((RET_HEADS, RET_D, RET_D), F32)],
        compiler_params=pltpu.CompilerParams(dimension_semantics=("arbitrary", "arbitrary"),
                                             vmem_limit_bytes=VMEM_LIMIT_BYTES),
    )(x, pn, w_r, cosf, sinf, dmat, qdec, kdec, cdec, ret_norm[None, :])

    n_tok = bsz * seq
    w_o = w_out.astype(BF16)
    tok = lambda w: pl.BlockSpec((tm, w), lambda i: (i, 0))
    full = lambda r, w: pl.BlockSpec((r, w), lambda i: (0, 0))
    y = pl.pallas_call(
        _out_kernel,
        grid=(n_tok // tm,),
        in_specs=[tok(D_MODEL), tok(GDN_W), tok(SSD_W), tok(RET_W),
                  full(GDN_W, D_MODEL), full(SSD_W, D_MODEL), full(RET_W, D_MODEL), full(1, D_MODEL)],
        out_specs=tok(D_MODEL),
        out_shape=jax.ShapeDtypeStruct((n_tok, D_MODEL), F32),
        compiler_params=pltpu.CompilerParams(dimension_semantics=("arbitrary",),
                                             vmem_limit_bytes=VMEM_LIMIT_BYTES),
    )(x.reshape(n_tok, D_MODEL), o_a.reshape(n_tok, GDN_W), o_b.reshape(n_tok, SSD_W),
      o_c.reshape(n_tok, RET_W), w_o[:GDN_W], w_o[GDN_W:GDN_W + SSD_W], w_o[GDN_W + SSD_W:], post_norm[None, :])
    return y.reshape(bsz, seq, D_MODEL)


def kernel(x, pre_norm, post_norm, w_in, gdn_conv, gdn_A_log, gdn_dt_bias, gdn_norm, ssd_conv, ssd_conv_b,
           ssd_A_log, ssd_dt_bias, ssd_D, ssd_norm, ret_norm, w_out):
    seq = x.shape[1]
    tb = min(512, seq)
    tm = min(512, x.shape[0] * seq)
    tables = _retention_tables(seq)
    for l in range(pre_norm.shape[0]):
        x = _layer(x, pre_norm[l], post_norm[l], w_in[l], gdn_conv[l], gdn_A_log[l], gdn_dt_bias[l], gdn_norm[l],
                   ssd_conv[l], ssd_conv_b[l], ssd_A_log[l], ssd_dt_bias[l], ssd_D[l], ssd_norm[l], ret_norm[l],
                   w_out[l], tables, tb, tm)
    return x
```

```python
import functools

import jax
import jax.numpy as jnp
from jax import lax
from jax.experimental import pallas as pl
from jax.experimental.pallas import tpu as pltpu

F32 = jnp.float32
BF16 = jnp.bfloat16

D_MODEL = 1024
CHUNK = 64
CONV_W = 4
CONV_PAD = 8
WAVE_CHUNKS = 2
PROJ_SLICE = 256
EPS = 1e-6
LANES = 128

GDN_HEADS, GDN_D = 4, 128
GDN_W = GDN_HEADS * GDN_D
SSD_HEADS, SSD_P, SSD_N, SSD_GROUPS = 16, 64, 128, 2
SSD_W = SSD_HEADS * SSD_P
SSD_GW = SSD_W // SSD_GROUPS
RET_HEADS, RET_D = 4, 128
RET_W = RET_HEADS * RET_D
ROPE_BASE = 10000.0
MIX_W = GDN_W + SSD_W + RET_W

GDN_SIZES = [GDN_W, GDN_W, GDN_W, GDN_W, GDN_HEADS, GDN_HEADS]
SSD_SIZES = [SSD_W, SSD_GROUPS * SSD_N, SSD_GROUPS * SSD_N, SSD_W, SSD_HEADS]
RET_SIZES = [RET_W, RET_W, RET_W, RET_W]
IN_SIZES = GDN_SIZES + SSD_SIZES + RET_SIZES

GDN_CONV_CH = 3 * GDN_W
SSD_CONV_CH = SSD_W + 2 * SSD_GROUPS * SSD_N
GDN_PW = 4 * GDN_W + LANES
SSD_PW = SSD_CONV_CH + SSD_W + LANES
RET_PW = 4 * RET_W
GDN_ALPHA_LANE = GDN_HEADS

VMEM_LIMIT_BYTES = 56 * 1024 * 1024


def _mm(a, b):
    return jnp.dot(a.astype(BF16), b.astype(BF16), preferred_element_type=F32)


def _mm_nt(a, b):
    return lax.dot_general(a.astype(BF16), b.astype(BF16), (((1,), (1,)), ((), ())),
                           preferred_element_type=F32)


def _mm_tn(a, b):
    return lax.dot_general(a.astype(BF16), b.astype(BF16), (((0,), (0,)), ((), ())),
                           preferred_element_type=F32)


def _split3(x):
    hi = x.astype(BF16)
    r = x - hi.astype(F32)
    mid = r.astype(BF16)
    lo = (r - mid.astype(F32)).astype(BF16)
    return hi, mid, lo


def _mm_sel(sel, x, pieces=3):
    sel = sel.astype(BF16)
    acc = None
    for piece in _split3(x)[:pieces]:
        d = jnp.dot(sel, piece, preferred_element_type=F32)
        acc = d if acc is None else acc + d
    return acc


def _mm_sel_r(x, sel, pieces=3):
    sel = sel.astype(BF16)
    acc = None
    for piece in _split3(x)[:pieces]:
        d = jnp.dot(piece, sel, preferred_element_type=F32)
        acc = d if acc is None else acc + d
    return acc


def _sigmoid(x):
    return 1.0 / (1.0 + jnp.exp(-x))


def _silu(x):
    return x * _sigmoid(x)


def _softplus(x):
    return jnp.maximum(x, 0.0) + jnp.log(1.0 + jnp.exp(-jnp.abs(x)))


def _rms(x, w):
    return x * lax.rsqrt(jnp.mean(x * x, axis=-1, keepdims=True) + EPS) * w


def _zip_rounds(*gens):
    gens = list(gens)
    while gens:
        alive = []
        for g in gens:
            try:
                next(g)
                alive.append(g)
            except StopIteration:
                pass
        gens = alive
        if gens:
            yield


def _run_with_side(main, side, main_per_side):
    n = 0
    for _ in main:
        n += 1
        if n % main_per_side == 0:
            next(side, None)
    for _ in side:
        pass


def _waves(tb, prep, phase_a, phase_b):
    prev = None
    for w in range(tb // (CHUNK * WAVE_CHUNKS)):
        probs = prep(w)
        res = {}
        if prev is None:
            yield from phase_a(probs, res)
        else:
            yield from _zip_rounds(phase_a(probs, res), phase_b(*prev))
        prev = (probs, res)
    yield from phase_b(*prev)


def _project_now(x, pn_ref, w_ref, p_ref, tb):
    h = _rms(x, pn_ref[...]).astype(BF16)
    p_ref[CONV_PAD:CONV_PAD + tb, :] = jnp.dot(h, w_ref[...], preferred_element_type=F32)


def _project_sliced(x_view, pn_ref, w_ref, h_ref, p_ref, tb):
    h_ref[...] = _rms(x_view[...], pn_ref[...]).astype(BF16)
    width = p_ref.shape[1]
    for j in range(0, width, PROJ_SLICE):
        wj = min(PROJ_SLICE, width - j)
        p_ref[CONV_PAD:CONV_PAD + tb, j:j + wj] = jnp.dot(h_ref[...], w_ref[:, j:j + wj],
                                                          preferred_element_type=F32)
        yield


def _two_blocks(block_fn, n_groups, x_ref, xn_ref, pn_ref, w_ref, h_ref, p0_ref, p1_ref, tb):
    b, t = pl.program_id(0), pl.program_id(1)
    seq_start = t == 0
    n_slices = -(-p0_ref.shape[1] // PROJ_SLICE)
    per_side = max(1, n_groups // n_slices)

    @pl.when(jnp.logical_and(b == 0, seq_start))
    def _():
        _project_now(x_ref[0:tb, :], pn_ref, w_ref, p0_ref, tb)

    @pl.when(seq_start)
    def _():
        p0_ref[0:CONV_PAD, :] = jnp.zeros((CONV_PAD, p0_ref.shape[1]), F32)

    @pl.when(jnp.logical_not(seq_start))
    def _():
        p0_ref[0:CONV_PAD, :] = p1_ref[tb:tb + CONV_PAD, :]

    _run_with_side(block_fn(p0_ref, 0),
                   _project_sliced(x_ref.at[tb:2 * tb, :], pn_ref, w_ref, h_ref, p1_ref, tb), per_side)
    p1_ref[0:CONV_PAD, :] = p0_ref[tb:tb + CONV_PAD, :]
    _run_with_side(block_fn(p1_ref, tb),
                   _project_sliced(xn_ref, pn_ref, w_ref, h_ref, p0_ref, tb), per_side)


def _conv_chunk(p_ref, base, width, cw):
    win = p_ref[base:base + CHUNK + CONV_PAD, 0:width]
    acc = win[CONV_PAD:CONV_PAD + CHUNK] * cw[CONV_W - 1:CONV_W]
    for i in range(CONV_W - 1):
        lo = CONV_PAD - (CONV_W - 1) + i
        acc = acc + win[lo:lo + CHUNK] * cw[i:i + 1]
    return acc


def _tri_masks():
    row = lax.broadcasted_iota(jnp.int32, (CHUNK, CHUNK), 0)
    col = lax.broadcasted_iota(jnp.int32, (CHUNK, CHUNK), 1)
    return row >= col, row > col


SOLVE_GROUPS = 7


def _solve_unit_lower(a_list, rhs_list):
    lane = lax.broadcasted_iota(jnp.int32, (CHUNK, 2 * CHUNK), 1)
    row = lax.broadcasted_iota(jnp.int32, (CHUNK, 2 * CHUNK), 0)
    left = lane < CHUNK
    zeros = jnp.zeros((CHUNK, 2 * CHUNK), F32)
    pts = [jnp.where(lane == row + CHUNK, 1.0, jnp.concatenate([-a, jnp.zeros_like(a)], axis=1)) for a in a_list]
    k = 1
    while k < CHUNK:
        news = [_mm(jnp.where(left, pt, 0.0), jnp.concatenate([pt, zeros], axis=0)) for pt in pts]
        yield
        pts = [new + jnp.where(left, 0.0, pt) for new, pt in zip(news, pts)]
        k *= 2
    sols = [_mm(jnp.where(left, 0.0, pt), jnp.concatenate([jnp.zeros_like(rhs), rhs], axis=0))
            for pt, rhs in zip(pts, rhs_list)]
    yield
    return sols


GDN_GROUPS = (1 + SOLVE_GROUPS) * 4 + 2 * WAVE_CHUNKS


def _gdn_kernel(x_ref, xn_ref, pn_ref, w_ref, cw_ref, alog_ref, dtb_ref, gn_ref, o_ref,
                h_ref, p0_ref, p1_ref, s_ref, *, tb):
    @pl.when(pl.program_id(1) == 0)
    def _():
        s_ref[...] = jnp.zeros_like(s_ref)

    causal, strict = _tri_masks()
    ltri = causal.astype(F32)
    cw = cw_ref[...]
    neg_a = -jnp.exp(alog_ref[...])
    dtb = dtb_ref[...]
    gn = gn_ref[...]
    heads = range(GDN_HEADS)
    state = [s_ref[h] for h in heads]

    def block(p_ref, out_row0):
        def prep(w):
            probs = []
            for u in range(WAVE_CHUNKS):
                base = (w * WAVE_CHUNKS + u) * CHUNK
                rows = slice(base + CONV_PAD, base + CONV_PAD + CHUNK)
                qkv = _silu(_conv_chunk(p_ref, base, GDN_CONV_CH, cw))
                z = p_ref[rows, GDN_CONV_CH:GDN_CONV_CH + GDN_W]
                gt = p_ref[rows, 4 * GDN_W:GDN_PW]
                beta = _sigmoid(gt)
                g = neg_a * _softplus(gt + dtb)
                gcum = _mm_sel(ltri, g)
                gcum_t = gcum.T
                egc = jnp.exp(gcum)
                for h in heads:
                    al = GDN_ALPHA_LANE + h
                    q = qkv[:, h * GDN_D:(h + 1) * GDN_D]
                    k = qkv[:, GDN_W + h * GDN_D:GDN_W + (h + 1) * GDN_D]
                    v = qkv[:, 2 * GDN_W + h * GDN_D:2 * GDN_W + (h + 1) * GDN_D]
                    q = q * lax.rsqrt(jnp.sum(q * q, axis=-1, keepdims=True) + EPS) * (GDN_D ** -0.5)
                    k = k * lax.rsqrt(jnp.sum(k * k, axis=-1, keepdims=True) + EPS)
                    bt = beta[:, h:h + 1]
                    gc = gcum[:, al:al + 1]
                    eg = egc[:, al:al + 1]
                    gl = gcum[CHUNK - 1:CHUNK, al:al + 1]
                    kb = k * bt
                    probs.append(dict(
                        row=out_row0 + base, q=q, k=k, kb=kb, qe=q * eg,
                        dec=jnp.exp(jnp.where(causal, gc - gcum_t[al:al + 1, :], -jnp.inf)),
                        rhs=jnp.concatenate([v * bt, kb * eg], axis=1),
                        kdec=k * jnp.exp(gl - gc), egl=jnp.exp(gl),
                        zh=z[:, h * GDN_D:(h + 1) * GDN_D]))
            return probs

        def phase_a(probs, res):
            kk = [_mm_nt(p["kb"], p["k"]) for p in probs]
            qk = [_mm_nt(p["q"], p["k"]) for p in probs]
            yield
            res["sols"] = yield from _solve_unit_lower(
                [jnp.where(strict, m * p["dec"], 0.0) for m, p in zip(kk, probs)], [p["rhs"] for p in probs])
            res["attn"] = [m * p["dec"] for m, p in zip(qk, probs)]

        def phase_b(probs, res):
            for u in range(WAVE_CHUNKS):
                sl = slice(u * GDN_HEADS, (u + 1) * GDN_HEADS)
                ps, so, at = probs[sl], res["sols"][sl], res["attn"][sl]
                ws = [_mm(so[h][:, GDN_D:], state[h]) for h in heads]
                qs = [_mm(ps[h]["qe"], state[h]) for h in heads]
                yield
                v_new = [so[h][:, :GDN_D] - ws[h] for h in heads]
                av = [_mm(at[h], v_new[h]) for h in heads]
                kv = [_mm_tn(ps[h]["kdec"], v_new[h]) for h in heads]
                yield
                for h in heads:
                    state[h] = state[h] * ps[h]["egl"] + kv[h]
                    o_ref[ps[h]["row"]:ps[h]["row"] + CHUNK, h * GDN_D:(h + 1) * GDN_D] = (
                        _rms(qs[h] + av[h], gn) * _silu(ps[h]["zh"]))

        return _waves(tb, prep, phase_a, phase_b)

    _two_blocks(block, GDN_GROUPS, x_ref, xn_ref, pn_ref, w_ref, h_ref, p0_ref, p1_ref, tb)
    for h in heads:
        s_ref[h] = state[h]


SSD_GROUPS_PER_BLOCK = 3 * 4 + WAVE_CHUNKS


def _ssd_kernel(x_ref, xn_ref, pn_ref, w_ref, cw_ref, cb_ref, alog_ref, dtb_ref, dx_ref, nrm_ref, o_ref,
                h_ref, p0_ref, p1_ref, hs_ref, *, tb):
    @pl.when(pl.program_id(1) == 0)
    def _():
        hs_ref[...] = jnp.zeros_like(hs_ref)

    causal, _ = _tri_masks()
    ltri = causal.astype(F32)
    expand = (lax.broadcasted_iota(jnp.int32, (LANES, SSD_W), 1) // SSD_P
              == lax.broadcasted_iota(jnp.int32, (LANES, SSD_W), 0)).astype(F32)
    cw = cw_ref[...]
    cbias = cb_ref[...]
    neg_a = -jnp.exp(alog_ref[...])
    dtb = dtb_ref[...]
    groups = range(SSD_GROUPS)
    hpg = SSD_HEADS // SSD_GROUPS
    state = [hs_ref[g] for g in groups]

    def block(p_ref, out_row0):
        def prep(w):
            chunks = []
            for u in range(WAVE_CHUNKS):
                base = (w * WAVE_CHUNKS + u) * CHUNK
                rows = slice(base + CONV_PAD, base + CONV_PAD + CHUNK)
                xbc = _silu(_conv_chunk(p_ref, base, SSD_CONV_CH, cw) + cbias)
                dt = _softplus(p_ref[rows, SSD_CONV_CH + SSD_W:SSD_PW] + dtb)
                chunks.append(dict(
                    row=out_row0 + base, xs=xbc[:, :SSD_W], dt=dt,
                    bg=[xbc[:, SSD_W + g * SSD_N:SSD_W + (g + 1) * SSD_N] for g in groups],
                    cg=[xbc[:, SSD_W + (SSD_GROUPS + g) * SSD_N:SSD_W + (SSD_GROUPS + g + 1) * SSD_N]
                        for g in groups],
                    z=p_ref[rows, SSD_CONV_CH:SSD_CONV_CH + SSD_W]))
            return chunks

        def phase_a(chunks, res):
            acums = [_mm_sel(ltri, c["dt"] * neg_a) for c in chunks]
            cbms = [[_mm_nt(c["cg"][g], c["bg"][g]) for g in groups] for c in chunks]
            dt_xs = [_mm_sel_r(c["dt"], expand, pieces=2) for c in chunks]
            yield
            res["ac_x"] = [_mm_sel_r(a, expand, pieces=2) for a in acums]
            lmats = []
            for c, acum, dt_x in zip(chunks, acums, dt_xs):
                acum_t = acum.T
                c["xdt"] = c["xs"] * dt_x
                lmats.append([jnp.exp(jnp.where(causal, acum[:, h:h + 1] - acum_t[h:h + 1, :], -jnp.inf))
                              for h in range(SSD_HEADS)])
            halves = []
            for half in range(2):
                hh = range(half * hpg, (half + 1) * hpg)
                halves.append([[_mm(cbm[h // hpg] * lm[h], c["xdt"][:, h * SSD_P:(h + 1) * SSD_P]) for h in hh]
                               for c, cbm, lm in zip(chunks, cbms, lmats)])
                yield
            res["intra"] = [jnp.concatenate(halves[0][i] + halves[1][i], axis=1) for i in range(len(chunks))]

        def phase_b(chunks, res):
            for c, ac_x, y_intra in zip(chunks, res["ac_x"], res["intra"]):
                eac_x = jnp.exp(ac_x)
                al_x = ac_x[CHUNK - 1:CHUNK, :]
                wx = jnp.exp(al_x - ac_x) * c["xdt"]
                inter = [_mm(c["cg"][g], state[g]) for g in groups]
                upd = [_mm_tn(c["bg"][g], wx[:, g * SSD_GW:(g + 1) * SSD_GW]) for g in groups]
                yield
                y = (y_intra + jnp.concatenate(inter, axis=1) * eac_x + c["xs"] * dx_ref[...]) * _silu(c["z"])
                for g in groups:
                    gs = slice(g * SSD_GW, (g + 1) * SSD_GW)
                    state[g] = state[g] * jnp.exp(al_x[:, gs]) + upd[g]
                    o_ref[c["row"]:c["row"] + CHUNK, gs] = _rms(y[:, gs], nrm_ref[:, gs])

        return _waves(tb, prep, phase_a, phase_b)

    _two_blocks(block, SSD_GROUPS_PER_BLOCK, x_ref, xn_ref, pn_ref, w_ref, h_ref, p0_ref, p1_ref, tb)
    for g in groups:
        hs_ref[g] = state[g]


RET_GROUPS = 2 * 4 + WAVE_CHUNKS


def _ret_kernel(x_ref, xn_ref, pn_ref, w_ref, cos_ref, sin_ref, dmat_ref, qdec_ref, kdec_ref, cdec_ref, nrm_ref,
                o_ref, h_ref, p0_ref, p1_ref, r_ref, *, tb):
    @pl.when(pl.program_id(1) == 0)
    def _():
        r_ref[...] = jnp.zeros_like(r_ref)

    nrm = nrm_ref[...]
    heads = range(RET_HEADS)
    state = [r_ref[h] for h in heads]

    def block(p_ref, out_row0):
        def prep(w):
            probs = []
            for u in range(WAVE_CHUNKS):
                base = (w * WAVE_CHUNKS + u) * CHUNK
                rows = slice(base + CONV_PAD, base + CONV_PAD + CHUNK)
                cosf = cos_ref[out_row0 + base:out_row0 + base + CHUNK, :]
                sinf = sin_ref[out_row0 + base:out_row0 + base + CHUNK, :]
                for h in heads:
                    hs = slice(h * RET_D, (h + 1) * RET_D)
                    q = p_ref[rows, h * RET_D:(h + 1) * RET_D]
                    k = p_ref[rows, RET_W + h * RET_D:RET_W + (h + 1) * RET_D]
                    q = q * cosf + pltpu.roll(q, RET_D // 2, 1) * sinf
                    k = (k * cosf + pltpu.roll(k, RET_D // 2, 1) * sinf) * (RET_D ** -0.5)
                    probs.append(dict(
                        row=out_row0 + base, h=h, hs=hs, q=q, k=k, kd=k * kdec_ref[:, hs],
                        v=p_ref[rows, 2 * RET_W + h * RET_D:2 * RET_W + (h + 1) * RET_D],
                        gate=p_ref[rows, 3 * RET_W + h * RET_D:3 * RET_W + (h + 1) * RET_D]))
            return probs

        def phase_a(probs, res):
            scores = [_mm_nt(p["q"], p["k"]) * dmat_ref[p["h"]] for p in probs]
            res["kv"] = [_mm_tn(p["kd"], p["v"]) for p in probs]
            yield
            res["intra"] = [_mm(s, p["v"]) for s, p in zip(scores, probs)]
            yield

        def phase_b(probs, res):
            for u in range(WAVE_CHUNKS):
                sl = slice(u * RET_HEADS, (u + 1) * RET_HEADS)
                ps, kv, intra = probs[sl], res["kv"][sl], res["intra"][sl]
                qr = [_mm(ps[h]["q"], state[h]) for h in heads]
                yield
                for h in heads:
                    hs = ps[h]["hs"]
                    state[h] = state[h] * cdec_ref[:, hs] + kv[h]
                    o = intra[h] + qr[h] * qdec_ref[:, hs]
                    o_ref[ps[h]["row"]:ps[h]["row"] + CHUNK, hs] = _rms(o, nrm) * _silu(ps[h]["gate"])

        return _waves(tb, prep, phase_a, phase_b)

    _two_blocks(block, RET_GROUPS, x_ref, xn_ref, pn_ref, w_ref, h_ref, p0_ref, p1_ref, tb)
    for h in heads:
        r_ref[h] = state[h]


def _out_kernel(x_ref, a_ref, b_ref, c_ref, wa_ref, wb_ref, wc_ref, pn_ref, o_ref):
    out = (jnp.dot(a_ref[...].astype(BF16), wa_ref[...], preferred_element_type=F32)
           + jnp.dot(b_ref[...].astype(BF16), wb_ref[...], preferred_element_type=F32)
           + jnp.dot(c_ref[...].astype(BF16), wc_ref[...], preferred_element_type=F32))
    o_ref[...] = x_ref[...] + _rms(out, pn_ref[...])


def _pad_lanes(t):
    return jnp.pad(t, ((0, 0), (0, LANES - t.shape[-1])))


def _split(t, sizes):
    out, start = [], 0
    for s in sizes:
        out.append(t[..., start:start + s])
        start += s
    return out


def _const_spec(shape):
    return pl.BlockSpec(shape, lambda b, t: (0,) * len(shape))


def _mixer_call(body, x, consts, const_specs, pw, out_w, state_shape, tb):
    bsz, seq, _ = x.shape
    steps = seq // (2 * tb)

    def next_block(b, t):
        last = t == steps - 1
        return (jnp.where(last, jnp.minimum(b + 1, bsz - 1), b), jnp.where(last, 0, 2 * (t + 1)), 0)

    in_specs = [pl.BlockSpec((None, 2 * tb, D_MODEL), lambda b, t: (b, t, 0)),
                pl.BlockSpec((None, tb, D_MODEL), next_block)] + const_specs
    return pl.pallas_call(
        functools.partial(body, tb=tb),
        grid=(bsz, steps),
        in_specs=in_specs,
        out_specs=pl.BlockSpec((None, 2 * tb, out_w), lambda b, t: (b, t, 0)),
        out_shape=jax.ShapeDtypeStruct((bsz, seq, out_w), F32),
        scratch_shapes=[pltpu.VMEM((tb, D_MODEL), BF16),
                        pltpu.VMEM((tb + CONV_PAD, pw), F32), pltpu.VMEM((tb + CONV_PAD, pw), F32),
                        pltpu.VMEM(state_shape, F32)],
        compiler_params=pltpu.CompilerParams(dimension_semantics=("arbitrary", "arbitrary"),
                                             vmem_limit_bytes=VMEM_LIMIT_BYTES),
    )(x, x, *consts)


def _retention_tables(seq):
    half = RET_D // 2
    pos = jnp.arange(seq, dtype=jnp.int32)
    inv = ROPE_BASE ** (-jnp.arange(half, dtype=F32) / half)
    ang = pos.astype(F32)[:, None] * inv[None, :]
    cos, sin = jnp.cos(ang), jnp.sin(ang)
    cosf = jnp.concatenate([cos, cos], axis=-1)
    sinf = jnp.concatenate([-sin, sin], axis=-1)
    lg = jnp.log(1.0 - 2.0 ** (-5.0 - jnp.arange(RET_HEADS, dtype=F32)))
    idx = jnp.arange(CHUNK, dtype=F32)
    rel = idx[:, None] - idx[None, :]
    dmat = jnp.where(rel[None] >= 0, jnp.exp(jnp.maximum(rel, 0.0)[None] * lg[:, None, None]), 0.0)
    qdec = jnp.exp((idx[:, None] + 1.0) * lg[None, :])
    kdec = jnp.exp((CHUNK - 1.0 - idx)[:, None] * lg[None, :])
    cdec = jnp.exp(CHUNK * lg)
    rep = lambda t: jnp.repeat(t, RET_D, axis=-1)
    return cosf, sinf, dmat, rep(qdec), rep(kdec), rep(cdec[None, :])


def _layer(x, pre_norm, post_norm, w_in, gdn_conv, gdn_A_log, gdn_dt_bias, gdn_norm, ssd_conv, ssd_conv_b,
           ssd_A_log, ssd_dt_bias, ssd_D, ssd_norm, ret_norm, w_out, tables, tb, tm):
    bsz, seq, _ = x.shape
    (gq, gk, gv, gz, gb, ga, sx, sb, sc, sz, sdt, rq, rk, rv, rg) = _split(w_in, IN_SIZES)
    w_g = jnp.concatenate([gq, gk, gv, gz, _pad_lanes(jnp.concatenate([gb, ga], axis=-1))], axis=-1).astype(BF16)
    w_s = jnp.concatenate([sx, sb, sc, sz, _pad_lanes(sdt)], axis=-1).astype(BF16)
    w_r = jnp.concatenate([rq, rk, rv, rg], axis=-1).astype(BF16)
    pn = pre_norm[None, :]
    cs = _const_spec

    g_alog = _pad_lanes(jnp.concatenate([jnp.zeros((GDN_HEADS,), F32), gdn_A_log])[None, :])
    g_dtb = _pad_lanes(jnp.concatenate([jnp.zeros((GDN_HEADS,), F32), gdn_dt_bias])[None, :])
    o_a = _mixer_call(
        _gdn_kernel, x,
        [pn, w_g, gdn_conv, g_alog, g_dtb, gdn_norm[None, :]],
        [cs((1, D_MODEL)), cs((D_MODEL, GDN_PW)), cs((CONV_W, GDN_CONV_CH)), cs((1, LANES)), cs((1, LANES)),
         cs((1, GDN_D))],
        GDN_PW, GDN_W, (GDN_HEADS, GDN_D, GDN_D), tb)

    o_b = _mixer_call(
        _ssd_kernel, x,
        [pn, w_s, ssd_conv, ssd_conv_b[None, :], _pad_lanes(ssd_A_log[None, :]), _pad_lanes(ssd_dt_bias[None, :]),
         jnp.repeat(ssd_D, SSD_P)[None, :], ssd_norm[None, :]],
        [cs((1, D_MODEL)), cs((D_MODEL, SSD_PW)), cs((CONV_W, SSD_CONV_CH)), cs((1, SSD_CONV_CH)), cs((1, LANES)),
         cs((1, LANES)), cs((1, SSD_W)), cs((1, SSD_W))],
        SSD_PW, SSD_W, (SSD_GROUPS, SSD_N, SSD_GW), tb)

    cosf, sinf, dmat, qdec, kdec, cdec = tables
    rope_spec = pl.BlockSpec((2 * tb, RET_D), lambda b, t: (t, 0))
    o_c = _mixer_call(
        _ret_kernel, x,
        [pn, w_r, cosf, sinf, dmat, qdec, kdec, cdec, ret_norm[None, :]],
        [cs((1, D_MODEL)), cs((D_MODEL, RET_PW)), rope_spec, rope_spec, cs((RET_HEADS, CHUNK, CHUNK)),
         cs((CHUNK, RET_W)), cs((CHUNK, RET_W)), cs((1, RET_W)), cs((1, RET_D))],
        RET_PW, RET_W, (RET_HEADS, RET_D, RET_D), tb)

    n_tok = bsz * seq
    w_o = w_out.astype(BF16)
    tok = lambda w: pl.BlockSpec((tm, w), lambda i: (i, 0))
    full = lambda r, w: pl.BlockSpec((r, w), lambda i: (0, 0))
    y = pl.pallas_call(
        _out_kernel,
        grid=(n_tok // tm,),
        in_specs=[tok(D_MODEL), tok(GDN_W), tok(SSD_W), tok(RET_W),
                  full(GDN_W, D_MODEL), full(SSD_W, D_MODEL), full(RET_W, D_MODEL), full(1, D_MODEL)],
        out_specs=tok(D_MODEL),
        out_shape=jax.ShapeDtypeStruct((n_tok, D_MODEL), F32),
        compiler_params=pltpu.CompilerParams(dimension_semantics=("arbitrary",),
                                             vmem_limit_bytes=VMEM_LIMIT_BYTES),
    )(x.reshape(n_tok, D_MODEL), o_a.reshape(n_tok, GDN_W), o_b.reshape(n_tok, SSD_W),
      o_c.reshape(n_tok, RET_W), w_o[:GDN_W], w_o[GDN_W:GDN_W + SSD_W], w_o[GDN_W + SSD_W:], post_norm[None, :])
    return y.reshape(bsz, seq, D_MODEL)


def kernel(x, pre_norm, post_norm, w_in, gdn_conv, gdn_A_log, gdn_dt_bias, gdn_norm, ssd_conv, ssd_conv_b,
           ssd_A_log, ssd_dt_bias, ssd_D, ssd_norm, ret_norm, w_out):
    seq = x.shape[1]
    tb = min(512, seq // 2)
    tm = min(512, x.shape[0] * seq)
    tables = _retention_tables(seq)
    for l in range(pre_norm.shape[0]):
        x = _layer(x, pre_norm[l], post_norm[l], w_in[l], gdn_conv[l], gdn_A_log[l], gdn_dt_bias[l], gdn_norm[l],
                   ssd_conv[l], ssd_conv_b[l], ssd_A_log[l], ssd_dt_bias[l], ssd_D[l], ssd_norm[l], ret_norm[l],
                   w_out[l], tables, tb, tm)
    return x
```

```python
import functools

import jax
import jax.numpy as jnp
from jax import lax
from jax.experimental import pallas as pl
from jax.experimental.pallas import tpu as pltpu

F32 = jnp.float32
BF16 = jnp.bfloat16

D_MODEL = 1024
CHUNK = 64
CONV_W = 4
CONV_PAD = 8
GDN_WAVE, SSD_WAVE, RET_WAVE = 4, 2, 2
PROJ_SLICE = 256
EPS = 1e-6
LANES = 128

GDN_HEADS, GDN_D = 4, 128
GDN_W = GDN_HEADS * GDN_D
SSD_HEADS, SSD_P, SSD_N, SSD_GROUPS = 16, 64, 128, 2
SSD_W = SSD_HEADS * SSD_P
SSD_GW = SSD_W // SSD_GROUPS
RET_HEADS, RET_D = 4, 128
RET_W = RET_HEADS * RET_D
ROPE_BASE = 10000.0
MIX_W = GDN_W + SSD_W + RET_W

GDN_SIZES = [GDN_W, GDN_W, GDN_W, GDN_W, GDN_HEADS, GDN_HEADS]
SSD_SIZES = [SSD_W, SSD_GROUPS * SSD_N, SSD_GROUPS * SSD_N, SSD_W, SSD_HEADS]
RET_SIZES = [RET_W, RET_W, RET_W, RET_W]
IN_SIZES = GDN_SIZES + SSD_SIZES + RET_SIZES

GDN_CONV_CH = 3 * GDN_W
SSD_CONV_CH = SSD_W + 2 * SSD_GROUPS * SSD_N
GDN_PW = 4 * GDN_W + LANES
SSD_PW = SSD_CONV_CH + SSD_W + LANES
RET_PW = 4 * RET_W
GDN_ALPHA_LANE = GDN_HEADS

VMEM_LIMIT_BYTES = 56 * 1024 * 1024


def _mm(a, b):
    return jnp.dot(a.astype(BF16), b.astype(BF16), preferred_element_type=F32)


def _mm_nt(a, b):
    return lax.dot_general(a.astype(BF16), b.astype(BF16), (((1,), (1,)), ((), ())),
                           preferred_element_type=F32)


def _mm_tn(a, b):
    return lax.dot_general(a.astype(BF16), b.astype(BF16), (((0,), (0,)), ((), ())),
                           preferred_element_type=F32)


def _split3(x):
    hi = x.astype(BF16)
    r = x - hi.astype(F32)
    mid = r.astype(BF16)
    lo = (r - mid.astype(F32)).astype(BF16)
    return hi, mid, lo


def _mm_sel(sel_tiled, x, pieces=3):
    return jnp.dot(sel_tiled, jnp.concatenate(_split3(x)[:pieces], axis=0), preferred_element_type=F32)


def _mm_sel_r(x, sel_stacked, pieces=2):
    return jnp.dot(jnp.concatenate(_split3(x)[:pieces], axis=1), sel_stacked, preferred_element_type=F32)


def _sigmoid(x):
    return 1.0 / (1.0 + jnp.exp(-x))


def _silu(x):
    return x * _sigmoid(x)


def _softplus(x):
    return jnp.maximum(x, 0.0) + jnp.log(1.0 + jnp.exp(-jnp.abs(x)))


def _rms(x, w):
    return x * lax.rsqrt(jnp.mean(x * x, axis=-1, keepdims=True) + EPS) * w


def _zip_rounds(*gens):
    gens = list(gens)
    while gens:
        alive = []
        for g in gens:
            try:
                next(g)
                alive.append(g)
            except StopIteration:
                pass
        gens = alive
        if gens:
            yield


def _run_with_side(main, side, main_per_side):
    n = 0
    for _ in main:
        n += 1
        if n % main_per_side == 0:
            next(side, None)
    for _ in side:
        pass


def _waves(n_waves, prep, phase_a, phase_b):
    prev = None
    for w in range(n_waves):
        probs = prep(w)
        res = {}
        if prev is None:
            yield from phase_a(probs, res)
        else:
            yield from _zip_rounds(phase_a(probs, res), phase_b(*prev))
        prev = (probs, res)
    yield from phase_b(*prev)


def _cols(p_ref, r0, c0, c1):
    tiles = [p_ref[j, r0:r0 + CHUNK, :] for j in range(c0 // LANES, c1 // LANES)]
    return tiles[0] if len(tiles) == 1 else jnp.concatenate(tiles, axis=1)


def _store_cols(p_ref, tb, c0, val):
    for o in range(0, val.shape[1], LANES):
        p_ref[(c0 + o) // LANES, CONV_PAD:CONV_PAD + tb, :] = val[:, o:o + LANES]


def _project_now(x, pn_ref, w_ref, p_ref, tb):
    h = _rms(x, pn_ref[...]).astype(BF16)
    _store_cols(p_ref, tb, 0, jnp.dot(h, w_ref[...], preferred_element_type=F32))


def _project_sliced(x_view, pn_ref, w_ref, h_ref, p_ref, tb):
    h_ref[...] = _rms(x_view[...], pn_ref[...]).astype(BF16)
    width = w_ref.shape[1]
    for j in range(0, width, PROJ_SLICE):
        wj = min(PROJ_SLICE, width - j)
        _store_cols(p_ref, tb, j, jnp.dot(h_ref[...], w_ref[:, j:j + wj], preferred_element_type=F32))
        yield


def _two_blocks(block_fn, n_groups, x_ref, xn_ref, pn_ref, w_ref, h_ref, p0_ref, p1_ref, tb):
    b, t = pl.program_id(0), pl.program_id(1)
    seq_start = t == 0
    n_slices = -(-w_ref.shape[1] // PROJ_SLICE)
    per_side = max(1, n_groups // n_slices)

    @pl.when(jnp.logical_and(b == 0, seq_start))
    def _():
        _project_now(x_ref[0:tb, :], pn_ref, w_ref, p0_ref, tb)

    @pl.when(seq_start)
    def _():
        p0_ref[:, 0:CONV_PAD, :] = jnp.zeros((p0_ref.shape[0], CONV_PAD, LANES), F32)

    @pl.when(jnp.logical_not(seq_start))
    def _():
        p0_ref[:, 0:CONV_PAD, :] = p1_ref[:, tb:tb + CONV_PAD, :]

    _run_with_side(block_fn(p0_ref, 0),
                   _project_sliced(x_ref.at[tb:2 * tb, :], pn_ref, w_ref, h_ref, p1_ref, tb), per_side)
    p1_ref[:, 0:CONV_PAD, :] = p0_ref[:, tb:tb + CONV_PAD, :]
    _run_with_side(block_fn(p1_ref, tb),
                   _project_sliced(xn_ref, pn_ref, w_ref, h_ref, p0_ref, tb), per_side)


def _conv_chunk(p_ref, base, width, cw):
    acc = _cols(p_ref, base + CONV_PAD, 0, width) * cw[CONV_W - 1:CONV_W]
    for i in range(CONV_W - 1):
        lo = CONV_PAD - (CONV_W - 1) + i
        acc = acc + _cols(p_ref, base + lo, 0, width) * cw[i:i + 1]
    return acc


def _tri_masks():
    row = lax.broadcasted_iota(jnp.int32, (CHUNK, CHUNK), 0)
    col = lax.broadcasted_iota(jnp.int32, (CHUNK, CHUNK), 1)
    return row >= col, row > col


def _tiled_ltri(pieces):
    row = lax.broadcasted_iota(jnp.int32, (CHUNK, pieces * CHUNK), 0)
    col = lax.broadcasted_iota(jnp.int32, (CHUNK, pieces * CHUNK), 1)
    return (row >= (col & (CHUNK - 1))).astype(BF16)


SOLVE_GROUPS = 7


def _solve_unit_lower(a_list, rhs_list):
    lane = lax.broadcasted_iota(jnp.int32, (CHUNK, 2 * CHUNK), 1)
    row = lax.broadcasted_iota(jnp.int32, (CHUNK, 2 * CHUNK), 0)
    left = lane < CHUNK
    zeros = jnp.zeros((CHUNK, 2 * CHUNK), F32)
    pts = [jnp.where(lane == row + CHUNK, 1.0, jnp.concatenate([-a, jnp.zeros_like(a)], axis=1)) for a in a_list]
    k = 1
    while k < CHUNK:
        news = [_mm(jnp.where(left, pt, 0.0), jnp.concatenate([pt, zeros], axis=0)) for pt in pts]
        yield
        pts = [new + jnp.where(left, 0.0, pt) for new, pt in zip(news, pts)]
        k *= 2
    sols = [_mm(jnp.where(left, 0.0, pt), jnp.concatenate([jnp.zeros_like(rhs), rhs], axis=0))
            for pt, rhs in zip(pts, rhs_list)]
    yield
    return sols


def _gdn_kernel(x_ref, xn_ref, pn_ref, w_ref, cw_ref, alog_ref, dtb_ref, gn_ref, o_ref,
                h_ref, p0_ref, p1_ref, s_ref, *, tb):
    @pl.when(pl.program_id(1) == 0)
    def _():
        s_ref[...] = jnp.zeros_like(s_ref)

    causal, strict = _tri_masks()
    ltri3 = _tiled_ltri(3)
    cw = cw_ref[...]
    neg_a = -jnp.exp(alog_ref[...])
    dtb = dtb_ref[...]
    gn = gn_ref[...]
    heads = range(GDN_HEADS)
    state = [s_ref[h] for h in heads]
    n_waves = tb // (CHUNK * GDN_WAVE)

    def block(p_ref, out_row0):
        def prep(w):
            probs = []
            for u in range(GDN_WAVE):
                base = (w * GDN_WAVE + u) * CHUNK
                r0 = base + CONV_PAD
                qkv = _silu(_conv_chunk(p_ref, base, GDN_CONV_CH, cw))
                z = _cols(p_ref, r0, GDN_CONV_CH, GDN_CONV_CH + GDN_W)
                gt = _cols(p_ref, r0, 4 * GDN_W, GDN_PW)
                beta = _sigmoid(gt)
                g = neg_a * _softplus(gt + dtb)
                gcum = _mm_sel(ltri3, g)
                gcum_t = gcum.T
                egc = jnp.exp(gcum)
                for h in heads:
                    al = GDN_ALPHA_LANE + h
                    q = qkv[:, h * GDN_D:(h + 1) * GDN_D]
                    k = qkv[:, GDN_W + h * GDN_D:GDN_W + (h + 1) * GDN_D]
                    v = qkv[:, 2 * GDN_W + h * GDN_D:2 * GDN_W + (h + 1) * GDN_D]
                    q = q * (lax.rsqrt(jnp.sum(q * q, axis=-1, keepdims=True) + EPS) * (GDN_D ** -0.5))
                    k = k * lax.rsqrt(jnp.sum(k * k, axis=-1, keepdims=True) + EPS)
                    bt = beta[:, h:h + 1]
                    gc = gcum[:, al:al + 1]
                    eg = egc[:, al:al + 1]
                    gl = gcum[CHUNK - 1:CHUNK, al:al + 1]
                    kb = k * bt
                    probs.append(dict(
                        row=out_row0 + base, q=q, k=k, kb=kb, qe=q * eg,
                        dec=jnp.exp(jnp.where(causal, gc - gcum_t[al:al + 1, :], -jnp.inf)),
                        rhs=jnp.concatenate([v * bt, kb * eg], axis=1),
                        kdec=k * jnp.exp(gl - gc), egl=jnp.exp(gl),
                        zh=z[:, h * GDN_D:(h + 1) * GDN_D]))
            return probs

        def phase_a(probs, res):
            kk = [_mm_nt(p["kb"], p["k"]) for p in probs]
            qk = [_mm_nt(p["q"], p["k"]) for p in probs]
            yield
            res["sols"] = yield from _solve_unit_lower(
                [jnp.where(strict, m * p["dec"], 0.0) for m, p in zip(kk, probs)], [p["rhs"] for p in probs])
            res["attn"] = [m * p["dec"] for m, p in zip(qk, probs)]

        def phase_b(probs, res):
            for u in range(GDN_WAVE):
                sl = slice(u * GDN_HEADS, (u + 1) * GDN_HEADS)
                ps, so, at = probs[sl], res["sols"][sl], res["attn"][sl]
                ws = [_mm(so[h][:, GDN_D:], state[h]) for h in heads]
                qs = [_mm(ps[h]["qe"], state[h]) for h in heads]
                yield
                v_new = [so[h][:, :GDN_D] - ws[h] for h in heads]
                av = [_mm(at[h], v_new[h]) for h in heads]
                kv = [_mm_tn(ps[h]["kdec"], v_new[h]) for h in heads]
                yield
                for h in heads:
                    state[h] = state[h] * ps[h]["egl"] + kv[h]
                    o_ref[ps[h]["row"]:ps[h]["row"] + CHUNK, h * GDN_D:(h + 1) * GDN_D] = (
                        _rms(qs[h] + av[h], gn) * _silu(ps[h]["zh"])).astype(o_ref.dtype)

        return _waves(n_waves, prep, phase_a, phase_b)

    n_groups = (1 + SOLVE_GROUPS) * n_waves + 2 * GDN_WAVE
    _two_blocks(block, n_groups, x_ref, xn_ref, pn_ref, w_ref, h_ref, p0_ref, p1_ref, tb)
    for h in heads:
        s_ref[h] = state[h]


def _ssd_kernel(x_ref, xn_ref, pn_ref, w_ref, cw_ref, cb_ref, alog_ref, dtb_ref, dx_ref, nrm_ref, o_ref,
                h_ref, p0_ref, p1_ref, hs_ref, ac_ref, dtx_ref, acx_ref, *, tb):
    @pl.when(pl.program_id(1) == 0)
    def _():
        hs_ref[...] = jnp.zeros_like(hs_ref)

    row = lax.broadcasted_iota(jnp.int32, (CHUNK, LANES), 0)
    lane = lax.broadcasted_iota(jnp.int32, (CHUNK, LANES), 1)
    causal2 = row >= (lane & (CHUNK - 1))
    even_head = lane < SSD_P
    ltri3 = _tiled_ltri(3)
    expand2 = ((lax.broadcasted_iota(jnp.int32, (2 * LANES, SSD_W), 0) & (LANES - 1))
               == lax.broadcasted_iota(jnp.int32, (2 * LANES, SSD_W), 1) // SSD_P).astype(BF16)
    cw = cw_ref[...]
    cbias = cb_ref[...]
    neg_a = -jnp.exp(alog_ref[...])
    dtb = dtb_ref[...]
    groups = range(SSD_GROUPS)
    pairs_per_group = SSD_HEADS // SSD_GROUPS // 2
    state = [hs_ref[g] for g in groups]
    n_waves = tb // (CHUNK * SSD_WAVE)

    def block(p_ref, out_row0):
        def expansions():
            dts = [_softplus(_cols(p_ref, c * CHUNK + CONV_PAD, SSD_CONV_CH + SSD_W, SSD_PW) + dtb)
                   for c in range(tb // CHUNK)]
            acums = [_mm_sel(ltri3, dt * neg_a) for dt in dts]
            dtx_ref[...] = _mm_sel_r(jnp.concatenate(dts, axis=0), expand2)
            yield
            ac_ref[...] = jnp.concatenate(acums, axis=0)
            acx_ref[...] = _mm_sel_r(ac_ref[...], expand2)
            yield

        def prep(w):
            chunks = []
            for u in range(SSD_WAVE):
                base = (w * SSD_WAVE + u) * CHUNK
                xbc = _silu(_conv_chunk(p_ref, base, SSD_CONV_CH, cw) + cbias)
                xs = xbc[:, :SSD_W]
                acum = ac_ref[base:base + CHUNK, :]
                pair_t = jnp.concatenate([acum, pltpu.roll(acum, LANES - 1, 1)], axis=0).T
                chunks.append(dict(
                    row=out_row0 + base, base=base, xs=xs, acum=acum, pair_t=pair_t,
                    xdt=xs * dtx_ref[base:base + CHUNK, :],
                    bg=[xbc[:, SSD_W + g * SSD_N:SSD_W + (g + 1) * SSD_N] for g in groups],
                    cg=[xbc[:, SSD_W + (SSD_GROUPS + g) * SSD_N:SSD_W + (SSD_GROUPS + g + 1) * SSD_N]
                        for g in groups],
                    z=_cols(p_ref, base + CONV_PAD, SSD_CONV_CH, SSD_CONV_CH + SSD_W)))
            return chunks

        def pair_product(c, cb2, m):
            acum = c["acum"]
            col = jnp.where(even_head, acum[:, 2 * m:2 * m + 1], acum[:, 2 * m + 1:2 * m + 2])
            lmat = jnp.exp(jnp.where(causal2, col - c["pair_t"][2 * m:2 * m + 1, :], -jnp.inf))
            x2 = c["xdt"][:, m * LANES:(m + 1) * LANES]
            rhs = jnp.concatenate([jnp.where(even_head, x2, 0.0), jnp.where(even_head, 0.0, x2)], axis=0)
            return _mm(cb2[m // pairs_per_group] * lmat, rhs)

        def phase_a(chunks, res):
            cb2s = [[_mm_nt(c["cg"][g], jnp.concatenate([c["bg"][g], c["bg"][g]], axis=0)) for g in groups]
                    for c in chunks]
            yield
            halves = []
            for half in range(2):
                ms = range(half * pairs_per_group, (half + 1) * pairs_per_group)
                halves.append([[pair_product(c, cb2, m) for m in ms] for c, cb2 in zip(chunks, cb2s)])
                yield
            res["intra"] = [jnp.concatenate(halves[0][i] + halves[1][i], axis=1) for i in range(len(chunks))]

        def phase_b(chunks, res):
            for c, y_intra in zip(chunks, res["intra"]):
                ac_x = acx_ref[c["base"]:c["base"] + CHUNK, :]
                eac_x = jnp.exp(ac_x)
                al_x = ac_x[CHUNK - 1:CHUNK, :]
                wx = jnp.exp(al_x - ac_x) * c["xdt"]
                inter = [_mm(c["cg"][g], state[g]) for g in groups]
                upd = [_mm_tn(c["bg"][g], wx[:, g * SSD_GW:(g + 1) * SSD_GW]) for g in groups]
                yield
                y = (y_intra + jnp.concatenate(inter, axis=1) * eac_x + c["xs"] * dx_ref[...]) * _silu(c["z"])
                for g in groups:
                    gs = slice(g * SSD_GW, (g + 1) * SSD_GW)
                    state[g] = state[g] * jnp.exp(al_x[:, gs]) + upd[g]
                    o_ref[c["row"]:c["row"] + CHUNK, gs] = _rms(y[:, gs], nrm_ref[:, gs]).astype(o_ref.dtype)

        def pipeline():
            yield from expansions()
            yield from _waves(n_waves, prep, phase_a, phase_b)

        return pipeline()

    n_groups = 2 + 3 * n_waves + SSD_WAVE
    _two_blocks(block, n_groups, x_ref, xn_ref, pn_ref, w_ref, h_ref, p0_ref, p1_ref, tb)
    for g in groups:
        hs_ref[g] = state[g]


def _ret_kernel(x_ref, xn_ref, pn_ref, w_ref, cos_ref, sin_ref, dmat_ref, qdec_ref, kdec_ref, cdec_ref, nrm_ref,
                o_ref, h_ref, p0_ref, p1_ref, r_ref, *, tb):
    @pl.when(pl.program_id(1) == 0)
    def _():
        r_ref[...] = jnp.zeros_like(r_ref)

    nrm = nrm_ref[...]
    heads = range(RET_HEADS)
    state = [r_ref[h] for h in heads]
    n_waves = tb // (CHUNK * RET_WAVE)

    def block(p_ref, out_row0):
        def prep(w):
            probs = []
            for u in range(RET_WAVE):
                base = (w * RET_WAVE + u) * CHUNK
                r0 = base + CONV_PAD
                cosf = cos_ref[out_row0 + base:out_row0 + base + CHUNK, :]
                sinf = sin_ref[out_row0 + base:out_row0 + base + CHUNK, :]
                for h in heads:
                    hs = slice(h * RET_D, (h + 1) * RET_D)
                    q = _cols(p_ref, r0, h * RET_D, (h + 1) * RET_D)
                    k = _cols(p_ref, r0, RET_W + h * RET_D, RET_W + (h + 1) * RET_D)
                    q = q * cosf + pltpu.roll(q, RET_D // 2, 1) * sinf
                    k = (k * cosf + pltpu.roll(k, RET_D // 2, 1) * sinf) * (RET_D ** -0.5)
                    probs.append(dict(
                        row=out_row0 + base, h=h, hs=hs, q=q, k=k, kd=k * kdec_ref[:, hs],
                        v=_cols(p_ref, r0, 2 * RET_W + h * RET_D, 2 * RET_W + (h + 1) * RET_D),
                        gate=_cols(p_ref, r0, 3 * RET_W + h * RET_D, 3 * RET_W + (h + 1) * RET_D)))
            return probs

        def phase_a(probs, res):
            scores = [_mm_nt(p["q"], p["k"]) * dmat_ref[p["h"]] for p in probs]
            res["kv"] = [_mm_tn(p["kd"], p["v"]) for p in probs]
            yield
            res["intra"] = [_mm(s, p["v"]) for s, p in zip(scores, probs)]
            yield

        def phase_b(probs, res):
            for u in range(RET_WAVE):
                sl = slice(u * RET_HEADS, (u + 1) * RET_HEADS)
                ps, kv, intra = probs[sl], res["kv"][sl], res["intra"][sl]
                qr = [_mm(ps[h]["q"], state[h]) for h in heads]
                yield
                for h in heads:
                    hs = ps[h]["hs"]
                    state[h] = state[h] * cdec_ref[:, hs] + kv[h]
                    o = intra[h] + qr[h] * qdec_ref[:, hs]
                    o_ref[ps[h]["row"]:ps[h]["row"] + CHUNK, hs] = (
                        _rms(o, nrm) * _silu(ps[h]["gate"])).astype(o_ref.dtype)

        return _waves(n_waves, prep, phase_a, phase_b)

    _two_blocks(block, 2 * n_waves + RET_WAVE, x_ref, xn_ref, pn_ref, w_ref, h_ref, p0_ref, p1_ref, tb)
    for h in heads:
        r_ref[h] = state[h]


def _out_kernel(x_ref, a_ref, b_ref, c_ref, wa_ref, wb_ref, wc_ref, pn_ref, o_ref):
    out = (jnp.dot(a_ref[...], wa_ref[...], preferred_element_type=F32)
           + jnp.dot(b_ref[...], wb_ref[...], preferred_element_type=F32)
           + jnp.dot(c_ref[...], wc_ref[...], preferred_element_type=F32))
    o_ref[...] = x_ref[...] + _rms(out, pn_ref[...])


def _pad_lanes(t):
    return jnp.pad(t, ((0, 0), (0, LANES - t.shape[-1])))


def _split(t, sizes):
    out, start = [], 0
    for s in sizes:
        out.append(t[..., start:start + s])
        start += s
    return out


def _const_spec(shape):
    return pl.BlockSpec(shape, lambda b, t: (0,) * len(shape))


def _mixer_call(body, x, consts, const_specs, pw, out_w, state_shape, tb, extra_scratch=()):
    bsz, seq, _ = x.shape
    steps = seq // (2 * tb)

    def next_block(b, t):
        last = t == steps - 1
        return (jnp.where(last, jnp.minimum(b + 1, bsz - 1), b), jnp.where(last, 0, 2 * (t + 1)), 0)

    in_specs = [pl.BlockSpec((None, 2 * tb, D_MODEL), lambda b, t: (b, t, 0)),
                pl.BlockSpec((None, tb, D_MODEL), next_block)] + const_specs
    return pl.pallas_call(
        functools.partial(body, tb=tb),
        grid=(bsz, steps),
        in_specs=in_specs,
        out_specs=pl.BlockSpec((None, 2 * tb, out_w), lambda b, t: (b, t, 0)),
        out_shape=jax.ShapeDtypeStruct((bsz, seq, out_w), BF16),
        scratch_shapes=[pltpu.VMEM((tb, D_MODEL), BF16),
                        pltpu.VMEM((pw // LANES, tb + CONV_PAD, LANES), F32),
                        pltpu.VMEM((pw // LANES, tb + CONV_PAD, LANES), F32),
                        pltpu.VMEM(state_shape, F32), *extra_scratch],
        compiler_params=pltpu.CompilerParams(dimension_semantics=("arbitrary", "arbitrary"),
                                             vmem_limit_bytes=VMEM_LIMIT_BYTES),
    )(x, x, *consts)


def _retention_tables(seq):
    half = RET_D // 2
    pos = jnp.arange(seq, dtype=jnp.int32)
    inv = ROPE_BASE ** (-jnp.arange(half, dtype=F32) / half)
    ang = pos.astype(F32)[:, None] * inv[None, :]
    cos, sin = jnp.cos(ang), jnp.sin(ang)
    cosf = jnp.concatenate([cos, cos], axis=-1)
    sinf = jnp.concatenate([-sin, sin], axis=-1)
    lg = jnp.log(1.0 - 2.0 ** (-5.0 - jnp.arange(RET_HEADS, dtype=F32)))
    idx = jnp.arange(CHUNK, dtype=F32)
    rel = idx[:, None] - idx[None, :]
    dmat = jnp.where(rel[None] >= 0, jnp.exp(jnp.maximum(rel, 0.0)[None] * lg[:, None, None]), 0.0)
    qdec = jnp.exp((idx[:, None] + 1.0) * lg[None, :])
    kdec = jnp.exp((CHUNK - 1.0 - idx)[:, None] * lg[None, :])
    cdec = jnp.exp(CHUNK * lg)
    rep = lambda t: jnp.repeat(t, RET_D, axis=-1)
    return cosf, sinf, dmat, rep(qdec), rep(kdec), rep(cdec[None, :])


def _layer(x, pre_norm, post_norm, w_in, gdn_conv, gdn_A_log, gdn_dt_bias, gdn_norm, ssd_conv, ssd_conv_b,
           ssd_A_log, ssd_dt_bias, ssd_D, ssd_norm, ret_norm, w_out, tables, tb, tm):
    bsz, seq, _ = x.shape
    (gq, gk, gv, gz, gb, ga, sx, sb, sc, sz, sdt, rq, rk, rv, rg) = _split(w_in, IN_SIZES)
    w_g = jnp.concatenate([gq, gk, gv, gz, _pad_lanes(jnp.concatenate([gb, ga], axis=-1))], axis=-1).astype(BF16)
    w_s = jnp.concatenate([sx, sb, sc, sz, _pad_lanes(sdt)], axis=-1).astype(BF16)
    w_r = jnp.concatenate([rq, rk, rv, rg], axis=-1).astype(BF16)
    pn = pre_norm[None, :]
    cs = _const_spec

    g_alog = _pad_lanes(jnp.concatenate([jnp.zeros((GDN_HEADS,), F32), gdn_A_log])[None, :])
    g_dtb = _pad_lanes(jnp.concatenate([jnp.zeros((GDN_HEADS,), F32), gdn_dt_bias])[None, :])
    o_a = _mixer_call(
        _gdn_kernel, x,
        [pn, w_g, gdn_conv, g_alog, g_dtb, gdn_norm[None, :]],
        [cs((1, D_MODEL)), cs((D_MODEL, GDN_PW)), cs((CONV_W, GDN_CONV_CH)), cs((1, LANES)), cs((1, LANES)),
         cs((1, GDN_D))],
        GDN_PW, GDN_W, (GDN_HEADS, GDN_D, GDN_D), tb)

    o_b = _mixer_call(
        _ssd_kernel, x,
        [pn, w_s, ssd_conv, ssd_conv_b[None, :], _pad_lanes(ssd_A_log[None, :]), _pad_lanes(ssd_dt_bias[None, :]),
         jnp.repeat(ssd_D, SSD_P)[None, :], ssd_norm[None, :]],
        [cs((1, D_MODEL)), cs((D_MODEL, SSD_PW)), cs((CONV_W, SSD_CONV_CH)), cs((1, SSD_CONV_CH)), cs((1, LANES)),
         cs((1, LANES)), cs((1, SSD_W)), cs((1, SSD_W))],
        SSD_PW, SSD_W, (SSD_GROUPS, SSD_N, SSD_GW), tb,
        extra_scratch=[pltpu.VMEM((tb, LANES), F32), pltpu.VMEM((tb, SSD_W), F32), pltpu.VMEM((tb, SSD_W), F32)])

    cosf, sinf, dmat, qdec, kdec, cdec = tables
    rope_spec = pl.BlockSpec((2 * tb, RET_D), lambda b, t: (t, 0))
    o_c = _mixer_call(
        _ret_kernel, x,
        [pn, w_r, cosf, sinf, dmat, qdec, kdec, cdec, ret_norm[None, :]],
        [cs((1, D_MODEL)), cs((D_MODEL, RET_PW)), rope_spec, rope_spec, cs((RET_HEADS, CHUNK, CHUNK)),
         cs((CHUNK, RET_W)), cs((CHUNK, RET_W)), cs((1, RET_W)), cs((1, RET_D))],
        RET_PW, RET_W, (RET_HEADS, RET_D, RET_D), tb)

    n_tok = bsz * seq
    w_o = w_out.astype(BF16)
    tok = lambda w: pl.BlockSpec((tm, w), lambda i: (i, 0))
    full = lambda r, w: pl.BlockSpec((r, w), lambda i: (0, 0))
    y = pl.pallas_call(
        _out_kernel,
        grid=(n_tok // tm,),
        in_specs=[tok(D_MODEL), tok(GDN_W), tok(SSD_W), tok(RET_W),
                  full(GDN_W, D_MODEL), full(SSD_W, D_MODEL), full(RET_W, D_MODEL), full(1, D_MODEL)],
        out_specs=tok(D_MODEL),
        out_shape=jax.ShapeDtypeStruct((n_tok, D_MODEL), F32),
        compiler_params=pltpu.CompilerParams(dimension_semantics=("arbitrary",),
                                             vmem_limit_bytes=VMEM_LIMIT_BYTES),
    )(x.reshape(n_tok, D_MODEL), o_a.reshape(n_tok, GDN_W), o_b.reshape(n_tok, SSD_W),
      o_c.reshape(n_tok, RET_W), w_o[:GDN_W], w_o[GDN_W:GDN_W + SSD_W], w_o[GDN_W + SSD_W:], post_norm[None, :])
    return y.reshape(bsz, seq, D_MODEL)


def kernel(x, pre_norm, post_norm, w_in, gdn_conv, gdn_A_log, gdn_dt_bias, gdn_norm, ssd_conv, ssd_conv_b,
           ssd_A_log, ssd_dt_bias, ssd_D, ssd_norm, ret_norm, w_out):
    seq = x.shape[1]
    tb = min(512, seq // 2)
    tm = min(1024, x.shape[0] * seq)
    tables = _retention_tables(seq)
    for l in range(pre_norm.shape[0]):
        x = _layer(x, pre_norm[l], post_norm[l], w_in[l], gdn_conv[l], gdn_A_log[l], gdn_dt_bias[l], gdn_norm[l],
                   ssd_conv[l], ssd_conv_b[l], ssd_A_log[l], ssd_dt_bias[l], ssd_D[l], ssd_norm[l], ret_norm[l],
                   w_out[l], tables, tb, tm)
    return x
```

```python
import functools

import jax
import jax.numpy as jnp
from jax import lax
from jax.experimental import pallas as pl
from jax.experimental.pallas import tpu as pltpu

F32 = jnp.float32
BF16 = jnp.bfloat16

D_MODEL = 1024
CHUNK = 64
CONV_W = 4
CONV_PAD = 8
GDN_WAVE, SSD_WAVE, RET_WAVE = 4, 2, 1
PROJ_SLICE = 256
EPS = 1e-6
LANES = 128

GDN_HEADS, GDN_D = 4, 128
GDN_W = GDN_HEADS * GDN_D
SSD_HEADS, SSD_P, SSD_N, SSD_GROUPS = 16, 64, 128, 2
SSD_W = SSD_HEADS * SSD_P
SSD_GW = SSD_W // SSD_GROUPS
RET_HEADS, RET_D = 4, 128
RET_W = RET_HEADS * RET_D
ROPE_BASE = 10000.0
MIX_W = GDN_W + SSD_W + RET_W

GDN_SIZES = [GDN_W, GDN_W, GDN_W, GDN_W, GDN_HEADS, GDN_HEADS]
SSD_SIZES = [SSD_W, SSD_GROUPS * SSD_N, SSD_GROUPS * SSD_N, SSD_W, SSD_HEADS]
RET_SIZES = [RET_W, RET_W, RET_W, RET_W]
IN_SIZES = GDN_SIZES + SSD_SIZES + RET_SIZES

GDN_CONV_CH = 3 * GDN_W
SSD_CONV_CH = SSD_W + 2 * SSD_GROUPS * SSD_N
GDN_PW = 4 * GDN_W + LANES
SSD_PW = SSD_CONV_CH + SSD_W + LANES
RET_PW = 4 * RET_W
GDN_ALPHA_LANE = GDN_HEADS

VMEM_LIMIT_BYTES = 60 * 1024 * 1024


def _mm(a, b):
    return jnp.dot(a.astype(BF16), b.astype(BF16), preferred_element_type=F32)


def _mm_nt(a, b):
    return lax.dot_general(a.astype(BF16), b.astype(BF16), (((1,), (1,)), ((), ())),
                           preferred_element_type=F32)


def _mm_tn(a, b):
    return lax.dot_general(a.astype(BF16), b.astype(BF16), (((0,), (0,)), ((), ())),
                           preferred_element_type=F32)


def _split3(x):
    hi = x.astype(BF16)
    r = x - hi.astype(F32)
    mid = r.astype(BF16)
    lo = (r - mid.astype(F32)).astype(BF16)
    return hi, mid, lo


def _mm_sel(sel_tiled, x, pieces=3):
    return jnp.dot(sel_tiled, jnp.concatenate(_split3(x)[:pieces], axis=0), preferred_element_type=F32)


def _mm_sel_r(x, sel_stacked, pieces=2):
    return jnp.dot(jnp.concatenate(_split3(x)[:pieces], axis=1), sel_stacked, preferred_element_type=F32)


def _sigmoid(x):
    return 1.0 / (1.0 + jnp.exp(-x))


def _silu(x):
    return x * _sigmoid(x)


def _softplus(x):
    return jnp.maximum(x, 0.0) + jnp.log(1.0 + jnp.exp(-jnp.abs(x)))


def _rms(x, w):
    return x * lax.rsqrt(jnp.mean(x * x, axis=-1, keepdims=True) + EPS) * w


def _zip_rounds(*gens):
    gens = list(gens)
    while gens:
        alive = []
        for g in gens:
            try:
                next(g)
                alive.append(g)
            except StopIteration:
                pass
        gens = alive
        if gens:
            yield


_EXHAUSTED = object()


def _zip_ratio(main, side, n_main, n_side):
    issued, i = 0, 0
    for _ in main:
        i += 1
        while issued * n_main < i * n_side and next(side, _EXHAUSTED) is not _EXHAUSTED:
            issued += 1
        yield
    for _ in side:
        yield


def _run(gen):
    for _ in gen:
        pass


def _waves(n_waves, prep, phase_a, phase_b):
    prev = None
    for w in range(n_waves):
        probs = prep(w)
        res = {}
        if prev is None:
            yield from phase_a(probs, res)
        else:
            yield from _zip_rounds(phase_a(probs, res), phase_b(*prev))
        prev = (probs, res)
    yield from phase_b(*prev)


def _cols(p_ref, r0, c0, c1):
    tiles = [p_ref[j, r0:r0 + CHUNK, :] for j in range(c0 // LANES, c1 // LANES)]
    return tiles[0] if len(tiles) == 1 else jnp.concatenate(tiles, axis=1)


def _store_cols(p_ref, tb, c0, val):
    for o in range(0, val.shape[1], LANES):
        p_ref[(c0 + o) // LANES, CONV_PAD:CONV_PAD + tb, :] = val[:, o:o + LANES]


def _project_now(x, pn_ref, w_ref, p_ref, tb):
    h = _rms(x, pn_ref[...]).astype(BF16)
    _store_cols(p_ref, tb, 0, jnp.dot(h, w_ref[...], preferred_element_type=F32))


def _project_sliced(x_view, pn_ref, w_ref, h_ref, p_ref, tb):
    h_ref[...] = _rms(x_view[...], pn_ref[...]).astype(BF16)
    width = w_ref.shape[1]
    for j in range(0, width, PROJ_SLICE):
        wj = min(PROJ_SLICE, width - j)
        _store_cols(p_ref, tb, j, jnp.dot(h_ref[...], w_ref[:, j:j + wj], preferred_element_type=F32))
        yield


def _two_blocks(block_fn, n_groups, x_ref, xn_ref, pn_ref, w_ref, h_ref, p0_ref, p1_ref, tb):
    b, t = pl.program_id(0), pl.program_id(1)
    seq_start = t == 0
    n_slices = -(-w_ref.shape[1] // PROJ_SLICE)

    @pl.when(jnp.logical_and(b == 0, seq_start))
    def _():
        _project_now(x_ref[0:tb, :], pn_ref, w_ref, p0_ref, tb)

    @pl.when(seq_start)
    def _():
        p0_ref[:, 0:CONV_PAD, :] = jnp.zeros((p0_ref.shape[0], CONV_PAD, LANES), F32)

    @pl.when(jnp.logical_not(seq_start))
    def _():
        p0_ref[:, 0:CONV_PAD, :] = p1_ref[:, tb:tb + CONV_PAD, :]

    _run(_zip_ratio(block_fn(p0_ref, 0),
                    _project_sliced(x_ref.at[tb:2 * tb, :], pn_ref, w_ref, h_ref, p1_ref, tb), n_groups, n_slices))
    p1_ref[:, 0:CONV_PAD, :] = p0_ref[:, tb:tb + CONV_PAD, :]
    _run(_zip_ratio(block_fn(p1_ref, tb),
                    _project_sliced(xn_ref, pn_ref, w_ref, h_ref, p0_ref, tb), n_groups, n_slices))


def _conv_chunk(p_ref, base, width, cw):
    acc = _cols(p_ref, base + CONV_PAD, 0, width) * cw[CONV_W - 1:CONV_W]
    for i in range(CONV_W - 1):
        lo = CONV_PAD - (CONV_W - 1) + i
        acc = acc + _cols(p_ref, base + lo, 0, width) * cw[i:i + 1]
    return acc


def _tri_masks():
    row = lax.broadcasted_iota(jnp.int32, (CHUNK, CHUNK), 0)
    col = lax.broadcasted_iota(jnp.int32, (CHUNK, CHUNK), 1)
    return row >= col, row > col


def _tiled_ltri(pieces):
    row = lax.broadcasted_iota(jnp.int32, (CHUNK, pieces * CHUNK), 0)
    col = lax.broadcasted_iota(jnp.int32, (CHUNK, pieces * CHUNK), 1)
    return (row >= (col & (CHUNK - 1))).astype(BF16)


SOLVE_GROUPS = 7


def _solve_unit_lower(a_list, rhs_list):
    lane = lax.broadcasted_iota(jnp.int32, (CHUNK, 2 * CHUNK), 1)
    row = lax.broadcasted_iota(jnp.int32, (CHUNK, 2 * CHUNK), 0)
    left = lane < CHUNK
    zeros = jnp.zeros((CHUNK, 2 * CHUNK), F32)
    pts = [jnp.where(lane == row + CHUNK, 1.0, jnp.concatenate([-a, jnp.zeros_like(a)], axis=1)) for a in a_list]
    k = 1
    while k < CHUNK:
        news = [_mm(jnp.where(left, pt, 0.0), jnp.concatenate([pt, zeros], axis=0)) for pt in pts]
        yield
        pts = [new + jnp.where(left, 0.0, pt) for new, pt in zip(news, pts)]
        k *= 2
    sols = [_mm(jnp.where(left, 0.0, pt), jnp.concatenate([jnp.zeros_like(rhs), rhs], axis=0))
            for pt, rhs in zip(pts, rhs_list)]
    yield
    return sols


def _gdn_consts(cw_ref, alog_ref, dtb_ref, gn_ref):
    causal, strict = _tri_masks()
    return dict(causal=causal, strict=strict, ltri3=_tiled_ltri(3), cw=cw_ref[...],
                neg_a=-jnp.exp(alog_ref[...]), dtb=dtb_ref[...], gn=gn_ref[...])


def _gdn_groups(tb):
    return (1 + SOLVE_GROUPS) * (tb // (CHUNK * GDN_WAVE)) + 2 * GDN_WAVE


def _gdn_waves(p_ref, o_ref, out_row0, tb, state, k):
    heads = range(GDN_HEADS)
    causal, strict = k["causal"], k["strict"]

    def prep(w):
        probs = []
        for u in range(GDN_WAVE):
            base = (w * GDN_WAVE + u) * CHUNK
            r0 = base + CONV_PAD
            qkv = _silu(_conv_chunk(p_ref, base, GDN_CONV_CH, k["cw"]))
            z = _cols(p_ref, r0, GDN_CONV_CH, GDN_CONV_CH + GDN_W)
            gt = _cols(p_ref, r0, 4 * GDN_W, GDN_PW)
            beta = _sigmoid(gt)
            g = k["neg_a"] * _softplus(gt + k["dtb"])
            gcum = _mm_sel(k["ltri3"], g)
            gcum_t = gcum.T
            egc = jnp.exp(gcum)
            for h in heads:
                al = GDN_ALPHA_LANE + h
                q = qkv[:, h * GDN_D:(h + 1) * GDN_D]
                kk = qkv[:, GDN_W + h * GDN_D:GDN_W + (h + 1) * GDN_D]
                v = qkv[:, 2 * GDN_W + h * GDN_D:2 * GDN_W + (h + 1) * GDN_D]
                q = q * (lax.rsqrt(jnp.sum(q * q, axis=-1, keepdims=True) + EPS) * (GDN_D ** -0.5))
                kk = kk * lax.rsqrt(jnp.sum(kk * kk, axis=-1, keepdims=True) + EPS)
                bt = beta[:, h:h + 1]
                gc = gcum[:, al:al + 1]
                eg = egc[:, al:al + 1]
                gl = gcum[CHUNK - 1:CHUNK, al:al + 1]
                kb = kk * bt
                probs.append(dict(
                    row=out_row0 + base, q=q, k=kk, kb=kb, qe=q * eg,
                    dec=jnp.exp(jnp.where(causal, gc - gcum_t[al:al + 1, :], -jnp.inf)),
                    rhs=jnp.concatenate([v * bt, kb * eg], axis=1),
                    kdec=kk * jnp.exp(gl - gc), egl=jnp.exp(gl),
                    zh=z[:, h * GDN_D:(h + 1) * GDN_D]))
        return probs

    def phase_a(probs, res):
        kk = [_mm_nt(p["kb"], p["k"]) for p in probs]
        qk = [_mm_nt(p["q"], p["k"]) for p in probs]
        yield
        res["sols"] = yield from _solve_unit_lower(
            [jnp.where(strict, m * p["dec"], 0.0) for m, p in zip(kk, probs)], [p["rhs"] for p in probs])
        res["attn"] = [m * p["dec"] for m, p in zip(qk, probs)]

    def phase_b(probs, res):
        for u in range(GDN_WAVE):
            sl = slice(u * GDN_HEADS, (u + 1) * GDN_HEADS)
            ps, so, at = probs[sl], res["sols"][sl], res["attn"][sl]
            ws = [_mm(so[h][:, GDN_D:], state[h]) for h in heads]
            qs = [_mm(ps[h]["qe"], state[h]) for h in heads]
            yield
            v_new = [so[h][:, :GDN_D] - ws[h] for h in heads]
            av = [_mm(at[h], v_new[h]) for h in heads]
            kv = [_mm_tn(ps[h]["kdec"], v_new[h]) for h in heads]
            yield
            for h in heads:
                state[h] = state[h] * ps[h]["egl"] + kv[h]
                o_ref[ps[h]["row"]:ps[h]["row"] + CHUNK, h * GDN_D:(h + 1) * GDN_D] = (
                    _rms(qs[h] + av[h], k["gn"]) * _silu(ps[h]["zh"])).astype(o_ref.dtype)

    return _waves(tb // (CHUNK * GDN_WAVE), prep, phase_a, phase_b)


def _ret_groups(tb):
    return 2 * (tb // (CHUNK * RET_WAVE)) + RET_WAVE


def _ret_waves(p_ref, c0, o_ref, out_row0, tb, state, k):
    heads = range(RET_HEADS)

    def prep(w):
        probs = []
        for u in range(RET_WAVE):
            base = (w * RET_WAVE + u) * CHUNK
            r0 = base + CONV_PAD
            cosf = k["cos"][out_row0 + base:out_row0 + base + CHUNK, :]
            sinf = k["sin"][out_row0 + base:out_row0 + base + CHUNK, :]
            for h in heads:
                hs = slice(h * RET_D, (h + 1) * RET_D)
                q = _cols(p_ref, r0, c0 + h * RET_D, c0 + (h + 1) * RET_D)
                kk = _cols(p_ref, r0, c0 + RET_W + h * RET_D, c0 + RET_W + (h + 1) * RET_D)
                q = q * cosf + pltpu.roll(q, RET_D // 2, 1) * sinf
                kk = (kk * cosf + pltpu.roll(kk, RET_D // 2, 1) * sinf) * (RET_D ** -0.5)
                probs.append(dict(
                    row=out_row0 + base, h=h, hs=hs, q=q, k=kk, kd=kk * k["kdec"][:, hs],
                    v=_cols(p_ref, r0, c0 + 2 * RET_W + h * RET_D, c0 + 2 * RET_W + (h + 1) * RET_D),
                    gate=_cols(p_ref, r0, c0 + 3 * RET_W + h * RET_D, c0 + 3 * RET_W + (h + 1) * RET_D)))
        return probs

    def phase_a(probs, res):
        scores = [_mm_nt(p["q"], p["k"]) * k["dmat"][p["h"]] for p in probs]
        res["kv"] = [_mm_tn(p["kd"], p["v"]) for p in probs]
        yield
        res["intra"] = [_mm(sc, p["v"]) for sc, p in zip(scores, probs)]
        yield

    def phase_b(probs, res):
        for u in range(RET_WAVE):
            sl = slice(u * RET_HEADS, (u + 1) * RET_HEADS)
            ps, kv, intra = probs[sl], res["kv"][sl], res["intra"][sl]
            qr = [_mm(ps[h]["q"], state[h]) for h in heads]
            yield
            for h in heads:
                hs = ps[h]["hs"]
                state[h] = state[h] * k["cdec"][:, hs] + kv[h]
                o = intra[h] + qr[h] * k["qdec"][:, hs]
                o_ref[ps[h]["row"]:ps[h]["row"] + CHUNK, hs] = (
                    _rms(o, k["nrm"]) * _silu(ps[h]["gate"])).astype(o_ref.dtype)

    return _waves(tb // (CHUNK * RET_WAVE), prep, phase_a, phase_b)


def _gdn_ret_kernel(x_ref, xn_ref, pn_ref, w_ref, cw_ref, alog_ref, dtb_ref, gn_ref,
                    cos_ref, sin_ref, dmat_ref, qdec_ref, kdec_ref, cdec_ref, rn_ref,
                    oa_ref, oc_ref, h_ref, p0_ref, p1_ref, s_ref, r_ref, *, tb):
    @pl.when(pl.program_id(1) == 0)
    def _():
        s_ref[...] = jnp.zeros_like(s_ref)
        r_ref[...] = jnp.zeros_like(r_ref)

    kg = _gdn_consts(cw_ref, alog_ref, dtb_ref, gn_ref)
    kr = dict(cos=cos_ref, sin=sin_ref, dmat=dmat_ref, qdec=qdec_ref, kdec=kdec_ref, cdec=cdec_ref,
              nrm=rn_ref[...])
    s_state = [s_ref[h] for h in range(GDN_HEADS)]
    r_state = [r_ref[h] for h in range(RET_HEADS)]

    def block(p_ref, out_row0):
        return _zip_ratio(_gdn_waves(p_ref, oa_ref, out_row0, tb, s_state, kg),
                          _ret_waves(p_ref, GDN_PW, oc_ref, out_row0, tb, r_state, kr),
                          _gdn_groups(tb), _ret_groups(tb))

    _two_blocks(block, _gdn_groups(tb), x_ref, xn_ref, pn_ref, w_ref, h_ref, p0_ref, p1_ref, tb)
    for h in range(GDN_HEADS):
        s_ref[h] = s_state[h]
    for h in range(RET_HEADS):
        r_ref[h] = r_state[h]


def _ssd_kernel(x_ref, xn_ref, pn_ref, w_ref, cw_ref, cb_ref, alog_ref, dtb_ref, dx_ref, nrm_ref, o_ref,
                h_ref, p0_ref, p1_ref, hs_ref, ac_ref, dtx_ref, acx_ref, *, tb):
    @pl.when(pl.program_id(1) == 0)
    def _():
        hs_ref[...] = jnp.zeros_like(hs_ref)

    row = lax.broadcasted_iota(jnp.int32, (CHUNK, LANES), 0)
    lane = lax.broadcasted_iota(jnp.int32, (CHUNK, LANES), 1)
    causal2 = row >= (lane & (CHUNK - 1))
    even_head = lane < SSD_P
    ltri3 = _tiled_ltri(3)
    expand2 = ((lax.broadcasted_iota(jnp.int32, (2 * LANES, SSD_W), 0) & (LANES - 1))
               == lax.broadcasted_iota(jnp.int32, (2 * LANES, SSD_W), 1) // SSD_P).astype(BF16)
    cw = cw_ref[...]
    cbias = cb_ref[...]
    neg_a = -jnp.exp(alog_ref[...])
    dtb = dtb_ref[...]
    groups = range(SSD_GROUPS)
    pairs_per_group = SSD_HEADS // SSD_GROUPS // 2
    state = [hs_ref[g] for g in groups]
    n_waves = tb // (CHUNK * SSD_WAVE)

    def block(p_ref, out_row0):
        def expansions():
            dts = [_softplus(_cols(p_ref, c * CHUNK + CONV_PAD, SSD_CONV_CH + SSD_W, SSD_PW) + dtb)
                   for c in range(tb // CHUNK)]
            acums = [_mm_sel(ltri3, dt * neg_a) for dt in dts]
            dtx_ref[...] = _mm_sel_r(jnp.concatenate(dts, axis=0), expand2)
            yield
            ac_ref[...] = jnp.concatenate(acums, axis=0)
            acx_ref[...] = _mm_sel_r(ac_ref[...], expand2)
            yield

        def prep(w):
            chunks = []
            for u in range(SSD_WAVE):
                base = (w * SSD_WAVE + u) * CHUNK
                xbc = _silu(_conv_chunk(p_ref, base, SSD_CONV_CH, cw) + cbias)
                xs = xbc[:, :SSD_W]
                acum = ac_ref[base:base + CHUNK, :]
                pair_t = jnp.concatenate([acum, pltpu.roll(acum, LANES - 1, 1)], axis=0).T
                chunks.append(dict(
                    row=out_row0 + base, base=base, xs=xs, acum=acum, pair_t=pair_t,
                    xdt=xs * dtx_ref[base:base + CHUNK, :],
                    bg=[xbc[:, SSD_W + g * SSD_N:SSD_W + (g + 1) * SSD_N] for g in groups],
                    cg=[xbc[:, SSD_W + (SSD_GROUPS + g) * SSD_N:SSD_W + (SSD_GROUPS + g + 1) * SSD_N]
                        for g in groups],
                    z=_cols(p_ref, base + CONV_PAD, SSD_CONV_CH, SSD_CONV_CH + SSD_W)))
            return chunks

        def pair_product(c, cb2, m):
            acum = c["acum"]
            col = jnp.where(even_head, acum[:, 2 * m:2 * m + 1], acum[:, 2 * m + 1:2 * m + 2])
            lmat = jnp.exp(jnp.where(causal2, col - c["pair_t"][2 * m:2 * m + 1, :], -jnp.inf))
            x2 = c["xdt"][:, m * LANES:(m + 1) * LANES]
            rhs = jnp.concatenate([jnp.where(even_head, x2, 0.0), jnp.where(even_head, 0.0, x2)], axis=0)
            return _mm(cb2[m // pairs_per_group] * lmat, rhs)

        def phase_a(chunks, res):
            cb2s = [[_mm_nt(c["cg"][g], jnp.concatenate([c["bg"][g], c["bg"][g]], axis=0)) for g in groups]
                    for c in chunks]
            yield
            halves = []
            for half in range(2):
                ms = range(half * pairs_per_group, (half + 1) * pairs_per_group)
                halves.append([[pair_product(c, cb2, m) for m in ms] for c, cb2 in zip(chunks, cb2s)])
                yield
            res["intra"] = [jnp.concatenate(halves[0][i] + halves[1][i], axis=1) for i in range(len(chunks))]

        def phase_b(chunks, res):
            for c, y_intra in zip(chunks, res["intra"]):
                ac_x = acx_ref[c["base"]:c["base"] + CHUNK, :]
                eac_x = jnp.exp(ac_x)
                al_x = ac_x[CHUNK - 1:CHUNK, :]
                wx = jnp.exp(al_x - ac_x) * c["xdt"]
                inter = [_mm(c["cg"][g], state[g]) for g in groups]
                upd = [_mm_tn(c["bg"][g], wx[:, g * SSD_GW:(g + 1) * SSD_GW]) for g in groups]
                yield
                y = (y_intra + jnp.concatenate(inter, axis=1) * eac_x + c["xs"] * dx_ref[...]) * _silu(c["z"])
                for g in groups:
                    gs = slice(g * SSD_GW, (g + 1) * SSD_GW)
                    state[g] = state[g] * jnp.exp(al_x[:, gs]) + upd[g]
                    o_ref[c["row"]:c["row"] + CHUNK, gs] = _rms(y[:, gs], nrm_ref[:, gs]).astype(o_ref.dtype)

        def pipeline():
            yield from expansions()
            yield from _waves(n_waves, prep, phase_a, phase_b)

        return pipeline()

    n_groups = 2 + 3 * n_waves + SSD_WAVE
    _two_blocks(block, n_groups, x_ref, xn_ref, pn_ref, w_ref, h_ref, p0_ref, p1_ref, tb)
    for g in groups:
        hs_ref[g] = state[g]


def _out_kernel(x_ref, a_ref, b_ref, c_ref, wa_ref, wb_ref, wc_ref, pn_ref, o_ref):
    out = (jnp.dot(a_ref[...], wa_ref[...], preferred_element_type=F32)
           + jnp.dot(b_ref[...], wb_ref[...], preferred_element_type=F32)
           + jnp.dot(c_ref[...], wc_ref[...], preferred_element_type=F32))
    o_ref[...] = x_ref[...] + _rms(out, pn_ref[...])


def _pad_lanes(t):
    return jnp.pad(t, ((0, 0), (0, LANES - t.shape[-1])))


def _split(t, sizes):
    out, start = [], 0
    for s in sizes:
        out.append(t[..., start:start + s])
        start += s
    return out


def _const_spec(shape):
    return pl.BlockSpec(shape, lambda b, t: (0,) * len(shape))


def _mixer_call(body, x, consts, const_specs, pw, out_ws, scratch, tb):
    bsz, seq, _ = x.shape
    steps = seq // (2 * tb)

    def next_block(b, t):
        last = t == steps - 1
        return (jnp.where(last, jnp.minimum(b + 1, bsz - 1), b), jnp.where(last, 0, 2 * (t + 1)), 0)

    in_specs = [pl.BlockSpec((None, 2 * tb, D_MODEL), lambda b, t: (b, t, 0)),
                pl.BlockSpec((None, tb, D_MODEL), next_block)] + const_specs
    return pl.pallas_call(
        functools.partial(body, tb=tb),
        grid=(bsz, steps),
        in_specs=in_specs,
        out_specs=[pl.BlockSpec((None, 2 * tb, w), lambda b, t: (b, t, 0)) for w in out_ws],
        out_shape=[jax.ShapeDtypeStruct((bsz, seq, w), BF16) for w in out_ws],
        scratch_shapes=[pltpu.VMEM((tb, D_MODEL), BF16),
                        pltpu.VMEM((pw // LANES, tb + CONV_PAD, LANES), F32),
                        pltpu.VMEM((pw // LANES, tb + CONV_PAD, LANES), F32), *scratch],
        compiler_params=pltpu.CompilerParams(dimension_semantics=("arbitrary", "arbitrary"),
                                             vmem_limit_bytes=VMEM_LIMIT_BYTES),
    )(x, x, *consts)


def _retention_tables(seq):
    half = RET_D // 2
    pos = jnp.arange(seq, dtype=jnp.int32)
    inv = ROPE_BASE ** (-jnp.arange(half, dtype=F32) / half)
    ang = pos.astype(F32)[:, None] * inv[None, :]
    cos, sin = jnp.cos(ang), jnp.sin(ang)
    cosf = jnp.concatenate([cos, cos], axis=-1)
    sinf = jnp.concatenate([-sin, sin], axis=-1)
    lg = jnp.log(1.0 - 2.0 ** (-5.0 - jnp.arange(RET_HEADS, dtype=F32)))
    idx = jnp.arange(CHUNK, dtype=F32)
    rel = idx[:, None] - idx[None, :]
    dmat = jnp.where(rel[None] >= 0, jnp.exp(jnp.maximum(rel, 0.0)[None] * lg[:, None, None]), 0.0)
    qdec = jnp.exp((idx[:, None] + 1.0) * lg[None, :])
    kdec = jnp.exp((CHUNK - 1.0 - idx)[:, None] * lg[None, :])
    cdec = jnp.exp(CHUNK * lg)
    rep = lambda t: jnp.repeat(t, RET_D, axis=-1)
    return cosf, sinf, dmat, rep(qdec), rep(kdec), rep(cdec[None, :])


def _layer(x, pre_norm, post_norm, w_in, gdn_conv, gdn_A_log, gdn_dt_bias, gdn_norm, ssd_conv, ssd_conv_b,
           ssd_A_log, ssd_dt_bias, ssd_D, ssd_norm, ret_norm, w_out, tables, tb, tm):
    bsz, seq, _ = x.shape
    (gq, gk, gv, gz, gb, ga, sx, sb, sc, sz, sdt, rq, rk, rv, rg) = _split(w_in, IN_SIZES)
    w_g = jnp.concatenate([gq, gk, gv, gz, _pad_lanes(jnp.concatenate([gb, ga], axis=-1))], axis=-1).astype(BF16)
    w_s = jnp.concatenate([sx, sb, sc, sz, _pad_lanes(sdt)], axis=-1).astype(BF16)
    w_r = jnp.concatenate([rq, rk, rv, rg], axis=-1).astype(BF16)
    pn = pre_norm[None, :]
    cs = _const_spec

    g_alog = _pad_lanes(jnp.concatenate([jnp.zeros((GDN_HEADS,), F32), gdn_A_log])[None, :])
    g_dtb = _pad_lanes(jnp.concatenate([jnp.zeros((GDN_HEADS,), F32), gdn_dt_bias])[None, :])
    cosf, sinf, dmat, qdec, kdec, cdec = tables
    rope_spec = pl.BlockSpec((2 * tb, RET_D), lambda b, t: (t, 0))
    once = dict(pipeline_mode=pl.Buffered(1))
    o_a, o_c = _mixer_call(
        _gdn_ret_kernel, x,
        [pn, jnp.concatenate([w_g, w_r], axis=-1), gdn_conv, g_alog, g_dtb, gdn_norm[None, :],
         cosf, sinf, dmat, qdec, kdec, cdec, ret_norm[None, :]],
        [cs((1, D_MODEL)), pl.BlockSpec((D_MODEL, GDN_PW + RET_PW), lambda b, t: (0, 0), **once),
         cs((CONV_W, GDN_CONV_CH)), cs((1, LANES)), cs((1, LANES)), cs((1, GDN_D)),
         rope_spec, rope_spec, cs((RET_HEADS, CHUNK, CHUNK)), cs((CHUNK, RET_W)), cs((CHUNK, RET_W)),
         cs((1, RET_W)), cs((1, RET_D))],
        GDN_PW + RET_PW, [GDN_W, RET_W],
        [pltpu.VMEM((GDN_HEADS, GDN_D, GDN_D), F32), pltpu.VMEM((RET_HEADS, RET_D, RET_D), F32)], tb)

    (o_b,) = _mixer_call(
        _ssd_kernel, x,
        [pn, w_s, ssd_conv, ssd_conv_b[None, :], _pad_lanes(ssd_A_log[None, :]), _pad_lanes(ssd_dt_bias[None, :]),
         jnp.repeat(ssd_D, SSD_P)[None, :], ssd_norm[None, :]],
        [cs((1, D_MODEL)), pl.BlockSpec((D_MODEL, SSD_PW), lambda b, t: (0, 0), **once),
         cs((CONV_W, SSD_CONV_CH)), cs((1, SSD_CONV_CH)), cs((1, LANES)), cs((1, LANES)), cs((1, SSD_W)),
         cs((1, SSD_W))],
        SSD_PW, [SSD_W],
        [pltpu.VMEM((SSD_GROUPS, SSD_N, SSD_GW), F32), pltpu.VMEM((tb, LANES), F32), pltpu.VMEM((tb, SSD_W), F32),
         pltpu.VMEM((tb, SSD_W), F32)], tb)

    n_tok = bsz * seq
    w_o = w_out.astype(BF16)
    tok = lambda w: pl.BlockSpec((tm, w), lambda i: (i, 0))
    full = lambda r, w: pl.BlockSpec((r, w), lambda i: (0, 0))
    y = pl.pallas_call(
        _out_kernel,
        grid=(n_tok // tm,),
        in_specs=[tok(D_MODEL), tok(GDN_W), tok(SSD_W), tok(RET_W),
                  full(GDN_W, D_MODEL), full(SSD_W, D_MODEL), full(RET_W, D_MODEL), full(1, D_MODEL)],
        out_specs=tok(D_MODEL),
        out_shape=jax.ShapeDtypeStruct((n_tok, D_MODEL), F32),
        compiler_params=pltpu.CompilerParams(dimension_semantics=("arbitrary",),
                                             vmem_limit_bytes=VMEM_LIMIT_BYTES),
    )(x.reshape(n_tok, D_MODEL), o_a.reshape(n_tok, GDN_W), o_b.reshape(n_tok, SSD_W),
      o_c.reshape(n_tok, RET_W), w_o[:GDN_W], w_o[GDN_W:GDN_W + SSD_W], w_o[GDN_W + SSD_W:], post_norm[None, :])
    return y.reshape(bsz, seq, D_MODEL)


def kernel(x, pre_norm, post_norm, w_in, gdn_conv, gdn_A_log, gdn_dt_bias, gdn_norm, ssd_conv, ssd_conv_b,
           ssd_A_log, ssd_dt_bias, ssd_D, ssd_norm, ret_norm, w_out):
    seq = x.shape[1]
    tb = min(512, seq // 2)
    tm = min(1024, x.shape[0] * seq)
    tables = _retention_tables(seq)
    for l in range(pre_norm.shape[0]):
        x = _layer(x, pre_norm[l], post_norm[l], w_in[l], gdn_conv[l], gdn_A_log[l], gdn_dt_bias[l], gdn_norm[l],
                   ssd_conv[l], ssd_conv_b[l], ssd_A_log[l], ssd_dt_bias[l], ssd_D[l], ssd_norm[l], ret_norm[l],
                   w_out[l], tables, tb, tm)
    return x
```

```python
import functools

import jax
import jax.numpy as jnp
from jax import lax
from jax.experimental import pallas as pl
from jax.experimental.pallas import tpu as pltpu

F32 = jnp.float32
BF16 = jnp.bfloat16

D_MODEL = 1024
CHUNK = 64
CONV_W = 4
CONV_PAD = 8
GDN_WAVE, SSD_WAVE, RET_WAVE = 4, 2, 1
PROJ_SLICE = 256
EPS = 1e-6
LANES = 128

GDN_HEADS, GDN_D = 4, 128
GDN_W = GDN_HEADS * GDN_D
SSD_HEADS, SSD_P, SSD_N, SSD_GROUPS = 16, 64, 128, 2
SSD_W = SSD_HEADS * SSD_P
SSD_GW = SSD_W // SSD_GROUPS
RET_HEADS, RET_D = 4, 128
RET_W = RET_HEADS * RET_D
ROPE_BASE = 10000.0
MIX_W = GDN_W + SSD_W + RET_W

GDN_SIZES = [GDN_W, GDN_W, GDN_W, GDN_W, GDN_HEADS, GDN_HEADS]
SSD_SIZES = [SSD_W, SSD_GROUPS * SSD_N, SSD_GROUPS * SSD_N, SSD_W, SSD_HEADS]
RET_SIZES = [RET_W, RET_W, RET_W, RET_W]
IN_SIZES = GDN_SIZES + SSD_SIZES + RET_SIZES

GDN_CONV_CH = 3 * GDN_W
SSD_CONV_CH = SSD_W + 2 * SSD_GROUPS * SSD_N
GDN_PW = 4 * GDN_W + LANES
SSD_PW = SSD_CONV_CH + SSD_W + LANES
RET_PW = 4 * RET_W
GDN_ALPHA_LANE = GDN_HEADS

VMEM_LIMIT_BYTES = 60 * 1024 * 1024


def _mm(a, b):
    return jnp.dot(a.astype(BF16), b.astype(BF16), preferred_element_type=F32)


def _mm_nt(a, b):
    return lax.dot_general(a.astype(BF16), b.astype(BF16), (((1,), (1,)), ((), ())),
                           preferred_element_type=F32)


def _mm_tn(a, b):
    return lax.dot_general(a.astype(BF16), b.astype(BF16), (((0,), (0,)), ((), ())),
                           preferred_element_type=F32)


def _split3(x):
    hi = x.astype(BF16)
    r = x - hi.astype(F32)
    mid = r.astype(BF16)
    lo = (r - mid.astype(F32)).astype(BF16)
    return hi, mid, lo


def _mm_sel(sel_tiled, x, pieces=3):
    return jnp.dot(sel_tiled, jnp.concatenate(_split3(x)[:pieces], axis=0), preferred_element_type=F32)


def _sigmoid(x):
    return 1.0 / (1.0 + jnp.exp(-x))


def _silu(x):
    return x * _sigmoid(x)


def _softplus(x):
    return jnp.maximum(x, 0.0) + jnp.log(1.0 + jnp.exp(-jnp.abs(x)))


def _rms(x, w):
    return x * lax.rsqrt(jnp.mean(x * x, axis=-1, keepdims=True) + EPS) * w


def _zip_rounds(*gens):
    gens = list(gens)
    while gens:
        alive = []
        for g in gens:
            try:
                next(g)
                alive.append(g)
            except StopIteration:
                pass
        gens = alive
        if gens:
            yield


_EXHAUSTED = object()


def _zip_ratio(main, side, n_main, n_side):
    issued, i = 0, 0
    for _ in main:
        i += 1
        while issued * n_main < i * n_side and next(side, _EXHAUSTED) is not _EXHAUSTED:
            issued += 1
        yield
    for _ in side:
        yield


def _run(gen):
    for _ in gen:
        pass


def _waves(n_waves, prep, phase_a, phase_b):
    prev = None
    for w in range(n_waves):
        probs = prep(w)
        res = {}
        if prev is None:
            yield from phase_a(probs, res)
        else:
            yield from _zip_rounds(phase_a(probs, res), phase_b(*prev))
        prev = (probs, res)
    yield from phase_b(*prev)


def _cols(p_ref, r0, c0, c1):
    tiles = [p_ref[j, r0:r0 + CHUNK, :] for j in range(c0 // LANES, c1 // LANES)]
    return tiles[0] if len(tiles) == 1 else jnp.concatenate(tiles, axis=1)


def _store_cols(p_ref, tb, c0, val):
    for o in range(0, val.shape[1], LANES):
        p_ref[(c0 + o) // LANES, CONV_PAD:CONV_PAD + tb, :] = val[:, o:o + LANES]


def _project_now(x, pn_ref, w_ref, p_ref, tb):
    h = _rms(x, pn_ref[...]).astype(BF16)
    _store_cols(p_ref, tb, 0, jnp.dot(h, w_ref[...], preferred_element_type=F32))


def _project_sliced(x_view, pn_ref, w_ref, h_ref, p_ref, tb):
    h_ref[...] = _rms(x_view[...], pn_ref[...]).astype(BF16)
    width = w_ref.shape[1]
    for j in range(0, width, PROJ_SLICE):
        wj = min(PROJ_SLICE, width - j)
        _store_cols(p_ref, tb, j, jnp.dot(h_ref[...], w_ref[:, j:j + wj], preferred_element_type=F32))
        yield


def _two_blocks(block_fn, n_groups, x_ref, xn_ref, pn_ref, w_ref, h_ref, p0_ref, p1_ref, tb):
    b, t = pl.program_id(0), pl.program_id(1)
    seq_start = t == 0
    n_slices = -(-w_ref.shape[1] // PROJ_SLICE)

    @pl.when(jnp.logical_and(b == 0, seq_start))
    def _():
        _project_now(x_ref[0:tb, :], pn_ref, w_ref, p0_ref, tb)

    @pl.when(seq_start)
    def _():
        p0_ref[:, 0:CONV_PAD, :] = jnp.zeros((p0_ref.shape[0], CONV_PAD, LANES), F32)

    @pl.when(jnp.logical_not(seq_start))
    def _():
        p0_ref[:, 0:CONV_PAD, :] = p1_ref[:, tb:tb + CONV_PAD, :]

    _run(_zip_ratio(block_fn(p0_ref, 0),
                    _project_sliced(x_ref.at[tb:2 * tb, :], pn_ref, w_ref, h_ref, p1_ref, tb), n_groups, n_slices))
    p1_ref[:, 0:CONV_PAD, :] = p0_ref[:, tb:tb + CONV_PAD, :]
    _run(_zip_ratio(block_fn(p1_ref, tb),
                    _project_sliced(xn_ref, pn_ref, w_ref, h_ref, p0_ref, tb), n_groups, n_slices))


def _conv_chunk(p_ref, base, width, cw):
    acc = _cols(p_ref, base + CONV_PAD, 0, width) * cw[CONV_W - 1:CONV_W]
    for i in range(CONV_W - 1):
        lo = CONV_PAD - (CONV_W - 1) + i
        acc = acc + _cols(p_ref, base + lo, 0, width) * cw[i:i + 1]
    return acc


def _tri_masks():
    row = lax.broadcasted_iota(jnp.int32, (CHUNK, CHUNK), 0)
    col = lax.broadcasted_iota(jnp.int32, (CHUNK, CHUNK), 1)
    return row >= col, row > col


def _tiled_ltri(pieces):
    row = lax.broadcasted_iota(jnp.int32, (CHUNK, pieces * CHUNK), 0)
    col = lax.broadcasted_iota(jnp.int32, (CHUNK, pieces * CHUNK), 1)
    return (row >= (col & (CHUNK - 1))).astype(BF16)


SOLVE_GROUPS = 7


def _solve_unit_lower(a_list, rhs_list):
    lane = lax.broadcasted_iota(jnp.int32, (CHUNK, 2 * CHUNK), 1)
    row = lax.broadcasted_iota(jnp.int32, (CHUNK, 2 * CHUNK), 0)
    left = lane < CHUNK
    zeros = jnp.zeros((CHUNK, 2 * CHUNK), F32)
    pts = [jnp.where(lane == row + CHUNK, 1.0, jnp.concatenate([-a, jnp.zeros_like(a)], axis=1)) for a in a_list]
    k = 1
    while k < CHUNK:
        news = [_mm(jnp.where(left, pt, 0.0), jnp.concatenate([pt, zeros], axis=0)) for pt in pts]
        yield
        pts = [new + jnp.where(left, 0.0, pt) for new, pt in zip(news, pts)]
        k *= 2
    sols = [_mm(jnp.where(left, 0.0, pt), jnp.concatenate([jnp.zeros_like(rhs), rhs], axis=0))
            for pt, rhs in zip(pts, rhs_list)]
    yield
    return sols


def _gdn_consts(cw_ref, alog_ref, dtb_ref, gn_ref):
    causal, strict = _tri_masks()
    return dict(causal=causal, strict=strict, ltri3=_tiled_ltri(3), cw=cw_ref[...],
                neg_a=-jnp.exp(alog_ref[...]), dtb=dtb_ref[...], gn=gn_ref[...])


def _gdn_groups(tb):
    return (1 + SOLVE_GROUPS) * (tb // (CHUNK * GDN_WAVE)) + 2 * GDN_WAVE


def _gdn_waves(p_ref, o_ref, out_row0, tb, state, k):
    heads = range(GDN_HEADS)
    causal, strict = k["causal"], k["strict"]

    def prep(w):
        probs = []
        for u in range(GDN_WAVE):
            base = (w * GDN_WAVE + u) * CHUNK
            r0 = base + CONV_PAD
            qkv = _silu(_conv_chunk(p_ref, base, GDN_CONV_CH, k["cw"]))
            z = _cols(p_ref, r0, GDN_CONV_CH, GDN_CONV_CH + GDN_W)
            gt = _cols(p_ref, r0, 4 * GDN_W, GDN_PW)
            beta = _sigmoid(gt)
            g = k["neg_a"] * _softplus(gt + k["dtb"])
            gcum = _mm_sel(k["ltri3"], g)
            gcum_t = gcum.T
            egc = jnp.exp(gcum)
            for h in heads:
                al = GDN_ALPHA_LANE + h
                q = qkv[:, h * GDN_D:(h + 1) * GDN_D]
                kk = qkv[:, GDN_W + h * GDN_D:GDN_W + (h + 1) * GDN_D]
                v = qkv[:, 2 * GDN_W + h * GDN_D:2 * GDN_W + (h + 1) * GDN_D]
                q = q * (lax.rsqrt(jnp.sum(q * q, axis=-1, keepdims=True) + EPS) * (GDN_D ** -0.5))
                kk = kk * lax.rsqrt(jnp.sum(kk * kk, axis=-1, keepdims=True) + EPS)
                bt = beta[:, h:h + 1]
                gc = gcum[:, al:al + 1]
                eg = egc[:, al:al + 1]
                gl = gcum[CHUNK - 1:CHUNK, al:al + 1]
                kb = kk * bt
                probs.append(dict(
                    row=out_row0 + base, q=q, k=kk, kb=kb, qe=q * eg,
                    dec=jnp.exp(jnp.where(causal, gc - gcum_t[al:al + 1, :], -jnp.inf)),
                    rhs=jnp.concatenate([v * bt, kb * eg], axis=1),
                    kdec=kk * jnp.exp(gl - gc), egl=jnp.exp(gl),
                    zh=z[:, h * GDN_D:(h + 1) * GDN_D]))
        return probs

    def phase_a(probs, res):
        kk = [_mm_nt(p["kb"], p["k"]) for p in probs]
        qk = [_mm_nt(p["q"], p["k"]) for p in probs]
        yield
        res["sols"] = yield from _solve_unit_lower(
            [jnp.where(strict, m * p["dec"], 0.0) for m, p in zip(kk, probs)], [p["rhs"] for p in probs])
        res["attn"] = [m * p["dec"] for m, p in zip(qk, probs)]

    def phase_b(probs, res):
        for u in range(GDN_WAVE):
            sl = slice(u * GDN_HEADS, (u + 1) * GDN_HEADS)
            ps, so, at = probs[sl], res["sols"][sl], res["attn"][sl]
            ws = [_mm(so[h][:, GDN_D:], state[h]) for h in heads]
            qs = [_mm(ps[h]["qe"], state[h]) for h in heads]
            yield
            v_new = [so[h][:, :GDN_D] - ws[h] for h in heads]
            av = [_mm(at[h], v_new[h]) for h in heads]
            kv = [_mm_tn(ps[h]["kdec"], v_new[h]) for h in heads]
            yield
            for h in heads:
                state[h] = state[h] * ps[h]["egl"] + kv[h]
                o_ref[ps[h]["row"]:ps[h]["row"] + CHUNK, h * GDN_D:(h + 1) * GDN_D] = (
                    _rms(qs[h] + av[h], k["gn"]) * _silu(ps[h]["zh"])).astype(o_ref.dtype)

    return _waves(tb // (CHUNK * GDN_WAVE), prep, phase_a, phase_b)


def _ret_groups(tb):
    return 2 * (tb // (CHUNK * RET_WAVE)) + RET_WAVE


def _ret_waves(p_ref, c0, o_ref, out_row0, tb, state, k):
    heads = range(RET_HEADS)

    def prep(w):
        probs = []
        for u in range(RET_WAVE):
            base = (w * RET_WAVE + u) * CHUNK
            r0 = base + CONV_PAD
            cosf = k["cos"][out_row0 + base:out_row0 + base + CHUNK, :]
            sinf = k["sin"][out_row0 + base:out_row0 + base + CHUNK, :]
            for h in heads:
                hs = slice(h * RET_D, (h + 1) * RET_D)
                q = _cols(p_ref, r0, c0 + h * RET_D, c0 + (h + 1) * RET_D)
                kk = _cols(p_ref, r0, c0 + RET_W + h * RET_D, c0 + RET_W + (h + 1) * RET_D)
                q = q * cosf + pltpu.roll(q, RET_D // 2, 1) * sinf
                kk = (kk * cosf + pltpu.roll(kk, RET_D // 2, 1) * sinf) * (RET_D ** -0.5)
                probs.append(dict(
                    row=out_row0 + base, h=h, hs=hs, q=q, k=kk, kd=kk * k["kdec"][:, hs],
                    v=_cols(p_ref, r0, c0 + 2 * RET_W + h * RET_D, c0 + 2 * RET_W + (h + 1) * RET_D),
                    gate=_cols(p_ref, r0, c0 + 3 * RET_W + h * RET_D, c0 + 3 * RET_W + (h + 1) * RET_D)))
        return probs

    def phase_a(probs, res):
        scores = [_mm_nt(p["q"], p["k"]) * k["dmat"][p["h"]] for p in probs]
        res["kv"] = [_mm_tn(p["kd"], p["v"]) for p in probs]
        yield
        res["intra"] = [_mm(sc, p["v"]) for sc, p in zip(scores, probs)]
        yield

    def phase_b(probs, res):
        for u in range(RET_WAVE):
            sl = slice(u * RET_HEADS, (u + 1) * RET_HEADS)
            ps, kv, intra = probs[sl], res["kv"][sl], res["intra"][sl]
            qr = [_mm(ps[h]["q"], state[h]) for h in heads]
            yield
            for h in heads:
                hs = ps[h]["hs"]
                state[h] = state[h] * k["cdec"][:, hs] + kv[h]
                o = intra[h] + qr[h] * k["qdec"][:, hs]
                o_ref[ps[h]["row"]:ps[h]["row"] + CHUNK, hs] = (
                    _rms(o, k["nrm"]) * _silu(ps[h]["gate"])).astype(o_ref.dtype)

    return _waves(tb // (CHUNK * RET_WAVE), prep, phase_a, phase_b)


def _gdn_ret_kernel(x_ref, xn_ref, pn_ref, w_ref, cw_ref, alog_ref, dtb_ref, gn_ref,
                    cos_ref, sin_ref, dmat_ref, qdec_ref, kdec_ref, cdec_ref, rn_ref,
                    oa_ref, oc_ref, h_ref, p0_ref, p1_ref, s_ref, r_ref, *, tb):
    @pl.when(pl.program_id(1) == 0)
    def _():
        s_ref[...] = jnp.zeros_like(s_ref)
        r_ref[...] = jnp.zeros_like(r_ref)

    kg = _gdn_consts(cw_ref, alog_ref, dtb_ref, gn_ref)
    kr = dict(cos=cos_ref, sin=sin_ref, dmat=dmat_ref, qdec=qdec_ref, kdec=kdec_ref, cdec=cdec_ref,
              nrm=rn_ref[...])
    s_state = [s_ref[h] for h in range(GDN_HEADS)]
    r_state = [r_ref[h] for h in range(RET_HEADS)]

    def block(p_ref, out_row0):
        return _zip_ratio(_gdn_waves(p_ref, oa_ref, out_row0, tb, s_state, kg),
                          _ret_waves(p_ref, GDN_PW, oc_ref, out_row0, tb, r_state, kr),
                          _gdn_groups(tb), _ret_groups(tb))

    _two_blocks(block, _gdn_groups(tb), x_ref, xn_ref, pn_ref, w_ref, h_ref, p0_ref, p1_ref, tb)
    for h in range(GDN_HEADS):
        s_ref[h] = s_state[h]
    for h in range(RET_HEADS):
        r_ref[h] = r_state[h]


def _ssd_kernel(x_ref, xn_ref, pn_ref, w_ref, cw_ref, cb_ref, alog_ref, dtb_ref, dx_ref, nrm_ref, o_ref,
                h_ref, p0_ref, p1_ref, hs_ref, ac_ref, dt_ref, *, tb):
    @pl.when(pl.program_id(1) == 0)
    def _():
        hs_ref[...] = jnp.zeros_like(hs_ref)

    row = lax.broadcasted_iota(jnp.int32, (CHUNK, LANES), 0)
    lane = lax.broadcasted_iota(jnp.int32, (CHUNK, LANES), 1)
    causal2 = row >= (lane & (CHUNK - 1))
    even_head = lane < SSD_P
    ltri3 = _tiled_ltri(3)
    cw = cw_ref[...]
    cbias = cb_ref[...]
    neg_a = -jnp.exp(alog_ref[...])
    dtb = dtb_ref[...]
    groups = range(SSD_GROUPS)
    pairs_per_group = SSD_HEADS // SSD_GROUPS // 2
    state = [hs_ref[g] for g in groups]
    n_waves = tb // (CHUNK * SSD_WAVE)

    def block(p_ref, out_row0):
        def cumulative():
            for c in range(tb // CHUNK):
                dt = _softplus(_cols(p_ref, c * CHUNK + CONV_PAD, SSD_CONV_CH + SSD_W, SSD_PW) + dtb)
                dt_ref[c * CHUNK:(c + 1) * CHUNK, :] = dt
                ac_ref[c * CHUNK:(c + 1) * CHUNK, :] = _mm_sel(ltri3, dt * neg_a)
            yield

        def per_channel(t):
            return [jnp.where(even_head, t[:, 2 * m:2 * m + 1], t[:, 2 * m + 1:2 * m + 2])
                    for m in range(SSD_HEADS // 2)]

        def prep(w):
            chunks = []
            for u in range(SSD_WAVE):
                base = (w * SSD_WAVE + u) * CHUNK
                xbc = _silu(_conv_chunk(p_ref, base, SSD_CONV_CH, cw) + cbias)
                xs = xbc[:, :SSD_W]
                acum = ac_ref[base:base + CHUNK, :]
                ac_cols = per_channel(acum)
                pair_t = jnp.concatenate([acum, pltpu.roll(acum, LANES - 1, 1)], axis=0).T
                chunks.append(dict(
                    row=out_row0 + base, base=base, xs=xs, ac_cols=ac_cols, pair_t=pair_t,
                    ac_x=jnp.concatenate(ac_cols, axis=1),
                    xdt=xs * jnp.concatenate(per_channel(dt_ref[base:base + CHUNK, :]), axis=1),
                    bg=[xbc[:, SSD_W + g * SSD_N:SSD_W + (g + 1) * SSD_N] for g in groups],
                    cg=[xbc[:, SSD_W + (SSD_GROUPS + g) * SSD_N:SSD_W + (SSD_GROUPS + g + 1) * SSD_N]
                        for g in groups],
                    z=_cols(p_ref, base + CONV_PAD, SSD_CONV_CH, SSD_CONV_CH + SSD_W)))
            return chunks

        def pair_product(c, cb2, m):
            lmat = jnp.exp(jnp.where(causal2, c["ac_cols"][m] - c["pair_t"][2 * m:2 * m + 1, :], -jnp.inf))
            x2 = c["xdt"][:, m * LANES:(m + 1) * LANES]
            rhs = jnp.concatenate([jnp.where(even_head, x2, 0.0), jnp.where(even_head, 0.0, x2)], axis=0)
            return _mm(cb2[m // pairs_per_group] * lmat, rhs)

        def phase_a(chunks, res):
            cb2s = [[_mm_nt(c["cg"][g], jnp.concatenate([c["bg"][g], c["bg"][g]], axis=0)) for g in groups]
                    for c in chunks]
            yield
            halves = []
            for half in range(2):
                ms = range(half * pairs_per_group, (half + 1) * pairs_per_group)
                halves.append([[pair_product(c, cb2, m) for m in ms] for c, cb2 in zip(chunks, cb2s)])
                yield
            res["intra"] = [jnp.concatenate(halves[0][i] + halves[1][i], axis=1) for i in range(len(chunks))]

        def phase_b(chunks, res):
            for c, y_intra in zip(chunks, res["intra"]):
                ac_x = c["ac_x"]
                eac_x = jnp.exp(ac_x)
                al_x = ac_x[CHUNK - 1:CHUNK, :]
                wx = jnp.exp(al_x - ac_x) * c["xdt"]
                inter = [_mm(c["cg"][g], state[g]) for g in groups]
                upd = [_mm_tn(c["bg"][g], wx[:, g * SSD_GW:(g + 1) * SSD_GW]) for g in groups]
                yield
                y = (y_intra + jnp.concatenate(inter, axis=1) * eac_x + c["xs"] * dx_ref[...]) * _silu(c["z"])
                for g in groups:
                    gs = slice(g * SSD_GW, (g + 1) * SSD_GW)
                    state[g] = state[g] * jnp.exp(al_x[:, gs]) + upd[g]
                    o_ref[c["row"]:c["row"] + CHUNK, gs] = _rms(y[:, gs], nrm_ref[:, gs]).astype(o_ref.dtype)

        def pipeline():
            yield from cumulative()
            yield from _waves(n_waves, prep, phase_a, phase_b)

        return pipeline()

    n_groups = 1 + 3 * n_waves + SSD_WAVE
    _two_blocks(block, n_groups, x_ref, xn_ref, pn_ref, w_ref, h_ref, p0_ref, p1_ref, tb)
    for g in groups:
        hs_ref[g] = state[g]


def _out_kernel(x_ref, a_ref, b_ref, c_ref, wa_ref, wb_ref, wc_ref, pn_ref, o_ref):
    out = (jnp.dot(a_ref[...], wa_ref[...], preferred_element_type=F32)
           + jnp.dot(b_ref[...], wb_ref[...], preferred_element_type=F32)
           + jnp.dot(c_ref[...], wc_ref[...], preferred_element_type=F32))
    o_ref[...] = x_ref[...] + _rms(out, pn_ref[...])


def _pad_lanes(t):
    return jnp.pad(t, ((0, 0), (0, LANES - t.shape[-1])))


def _split(t, sizes):
    out, start = [], 0
    for s in sizes:
        out.append(t[..., start:start + s])
        start += s
    return out


def _const_spec(shape):
    return pl.BlockSpec(shape, lambda b, t: (0,) * len(shape))


def _mixer_call(body, x, consts, const_specs, pw, out_ws, scratch, tb):
    bsz, seq, _ = x.shape
    steps = seq // (2 * tb)

    def next_block(b, t):
        last = t == steps - 1
        return (jnp.where(last, jnp.minimum(b + 1, bsz - 1), b), jnp.where(last, 0, 2 * (t + 1)), 0)

    in_specs = [pl.BlockSpec((None, 2 * tb, D_MODEL), lambda b, t: (b, t, 0)),
                pl.BlockSpec((None, tb, D_MODEL), next_block)] + const_specs
    return pl.pallas_call(
        functools.partial(body, tb=tb),
        grid=(bsz, steps),
        in_specs=in_specs,
        out_specs=[pl.BlockSpec((None, 2 * tb, w), lambda b, t: (b, t, 0)) for w in out_ws],
        out_shape=[jax.ShapeDtypeStruct((bsz, seq, w), BF16) for w in out_ws],
        scratch_shapes=[pltpu.VMEM((tb, D_MODEL), BF16),
                        pltpu.VMEM((pw // LANES, tb + CONV_PAD, LANES), F32),
                        pltpu.VMEM((pw // LANES, tb + CONV_PAD, LANES), F32), *scratch],
        compiler_params=pltpu.CompilerParams(dimension_semantics=("arbitrary", "arbitrary"),
                                             vmem_limit_bytes=VMEM_LIMIT_BYTES),
    )(x, x, *consts)


def _retention_tables(seq):
    half = RET_D // 2
    pos = jnp.arange(seq, dtype=jnp.int32)
    inv = ROPE_BASE ** (-jnp.arange(half, dtype=F32) / half)
    ang = pos.astype(F32)[:, None] * inv[None, :]
    cos, sin = jnp.cos(ang), jnp.sin(ang)
    cosf = jnp.concatenate([cos, cos], axis=-1)
    sinf = jnp.concatenate([-sin, sin], axis=-1)
    lg = jnp.log(1.0 - 2.0 ** (-5.0 - jnp.arange(RET_HEADS, dtype=F32)))
    idx = jnp.arange(CHUNK, dtype=F32)
    rel = idx[:, None] - idx[None, :]
    dmat = jnp.where(rel[None] >= 0, jnp.exp(jnp.maximum(rel, 0.0)[None] * lg[:, None, None]), 0.0)
    qdec = jnp.exp((idx[:, None] + 1.0) * lg[None, :])
    kdec = jnp.exp((CHUNK - 1.0 - idx)[:, None] * lg[None, :])
    cdec = jnp.exp(CHUNK * lg)
    rep = lambda t: jnp.repeat(t, RET_D, axis=-1)
    return cosf, sinf, dmat, rep(qdec), rep(kdec), rep(cdec[None, :])


def _layer(x, pre_norm, post_norm, w_in, gdn_conv, gdn_A_log, gdn_dt_bias, gdn_norm, ssd_conv, ssd_conv_b,
           ssd_A_log, ssd_dt_bias, ssd_D, ssd_norm, ret_norm, w_out, tables, tb, tm):
    bsz, seq, _ = x.shape
    (gq, gk, gv, gz, gb, ga, sx, sb, sc, sz, sdt, rq, rk, rv, rg) = _split(w_in, IN_SIZES)
    w_g = jnp.concatenate([gq, gk, gv, gz, _pad_lanes(jnp.concatenate([gb, ga], axis=-1))], axis=-1).astype(BF16)
    w_s = jnp.concatenate([sx, sb, sc, sz, _pad_lanes(sdt)], axis=-1).astype(BF16)
    w_r = jnp.concatenate([rq, rk, rv, rg], axis=-1).astype(BF16)
    pn = pre_norm[None, :]
    cs = _const_spec

    g_alog = _pad_lanes(jnp.concatenate([jnp.zeros((GDN_HEADS,), F32), gdn_A_log])[None, :])
    g_dtb = _pad_lanes(jnp.concatenate([jnp.zeros((GDN_HEADS,), F32), gdn_dt_bias])[None, :])
    cosf, sinf, dmat, qdec, kdec, cdec = tables
    rope_spec = pl.BlockSpec((2 * tb, RET_D), lambda b, t: (t, 0))
    once = dict(pipeline_mode=pl.Buffered(1))
    o_a, o_c = _mixer_call(
        _gdn_ret_kernel, x,
        [pn, jnp.concatenate([w_g, w_r], axis=-1), gdn_conv, g_alog, g_dtb, gdn_norm[None, :],
         cosf, sinf, dmat, qdec, kdec, cdec, ret_norm[None, :]],
        [cs((1, D_MODEL)), pl.BlockSpec((D_MODEL, GDN_PW + RET_PW), lambda b, t: (0, 0), **once),
         cs((CONV_W, GDN_CONV_CH)), cs((1, LANES)), cs((1, LANES)), cs((1, GDN_D)),
         rope_spec, rope_spec, cs((RET_HEADS, CHUNK, CHUNK)), cs((CHUNK, RET_W)), cs((CHUNK, RET_W)),
         cs((1, RET_W)), cs((1, RET_D))],
        GDN_PW + RET_PW, [GDN_W, RET_W],
        [pltpu.VMEM((GDN_HEADS, GDN_D, GDN_D), F32), pltpu.VMEM((RET_HEADS, RET_D, RET_D), F32)], tb)

    (o_b,) = _mixer_call(
        _ssd_kernel, x,
        [pn, w_s, ssd_conv, ssd_conv_b[None, :], _pad_lanes(ssd_A_log[None, :]), _pad_lanes(ssd_dt_bias[None, :]),
         jnp.repeat(ssd_D, SSD_P)[None, :], ssd_norm[None, :]],
        [cs((1, D_MODEL)), pl.BlockSpec((D_MODEL, SSD_PW), lambda b, t: (0, 0), **once),
         cs((CONV_W, SSD_CONV_CH)), cs((1, SSD_CONV_CH)), cs((1, LANES)), cs((1, LANES)), cs((1, SSD_W)),
         cs((1, SSD_W))],
        SSD_PW, [SSD_W],
        [pltpu.VMEM((SSD_GROUPS, SSD_N, SSD_GW), F32), pltpu.VMEM((tb, LANES), F32), pltpu.VMEM((tb, LANES), F32)],
        tb)

    n_tok = bsz * seq
    w_o = w_out.astype(BF16)
    tok = lambda w: pl.BlockSpec((tm, w), lambda i: (i, 0))
    full = lambda r, w: pl.BlockSpec((r, w), lambda i: (0, 0))
    y = pl.pallas_call(
        _out_kernel,
        grid=(n_tok // tm,),
        in_specs=[tok(D_MODEL), tok(GDN_W), tok(SSD_W), tok(RET_W),
                  full(GDN_W, D_MODEL), full(SSD_W, D_MODEL), full(RET_W, D_MODEL), full(1, D_MODEL)],
        out_specs=tok(D_MODEL),
        out_shape=jax.ShapeDtypeStruct((n_tok, D_MODEL), F32),
        compiler_params=pltpu.CompilerParams(dimension_semantics=("arbitrary",),
                                             vmem_limit_bytes=VMEM_LIMIT_BYTES),
    )(x.reshape(n_tok, D_MODEL), o_a.reshape(n_tok, GDN_W), o_b.reshape(n_tok, SSD_W),
      o_c.reshape(n_tok, RET_W), w_o[:GDN_W], w_o[GDN_W:GDN_W + SSD_W], w_o[GDN_W + SSD_W:], post_norm[None, :])
    return y.reshape(bsz, seq, D_MODEL)


def kernel(x, pre_norm, post_norm, w_in, gdn_conv, gdn_A_log, gdn_dt_bias, gdn_norm, ssd_conv, ssd_conv_b,
           ssd_A_log, ssd_dt_bias, ssd_D, ssd_norm, ret_norm, w_out):
    seq = x.shape[1]
    tb = min(512, seq // 2)
    tm = min(1024, x.shape[0] * seq)
    tables = _retention_tables(seq)
    for l in range(pre_norm.shape[0]):
        x = _layer(x, pre_norm[l], post_norm[l], w_in[l], gdn_conv[l], gdn_A_log[l], gdn_dt_bias[l], gdn_norm[l],
                   ssd_conv[l], ssd_conv_b[l], ssd_A_log[l], ssd_dt_bias[l], ssd_D[l], ssd_norm[l], ret_norm[l],
                   w_out[l], tables, tb, tm)
    return x
```

```python
import functools

import jax
import jax.numpy as jnp
from jax import lax
from jax.experimental import pallas as pl
from jax.experimental.pallas import tpu as pltpu

F32 = jnp.float32
BF16 = jnp.bfloat16

D_MODEL = 1024
CHUNK = 64
CONV_W = 4
CONV_PAD = 8
GDN_WAVE, SSD_WAVE, RET_WAVE = 4, 2, 1
PROJ_SLICE = 256
EPS = 1e-6
LANES = 128

GDN_HEADS, GDN_D = 4, 128
GDN_W = GDN_HEADS * GDN_D
SSD_HEADS, SSD_P, SSD_N, SSD_GROUPS = 16, 64, 128, 2
SSD_W = SSD_HEADS * SSD_P
SSD_GW = SSD_W // SSD_GROUPS
RET_HEADS, RET_D = 4, 128
RET_W = RET_HEADS * RET_D
ROPE_BASE = 10000.0
MIX_W = GDN_W + SSD_W + RET_W

GDN_SIZES = [GDN_W, GDN_W, GDN_W, GDN_W, GDN_HEADS, GDN_HEADS]
SSD_SIZES = [SSD_W, SSD_GROUPS * SSD_N, SSD_GROUPS * SSD_N, SSD_W, SSD_HEADS]
RET_SIZES = [RET_W, RET_W, RET_W, RET_W]
IN_SIZES = GDN_SIZES + SSD_SIZES + RET_SIZES

GDN_CONV_CH = 3 * GDN_W
SSD_CONV_CH = SSD_W + 2 * SSD_GROUPS * SSD_N
GDN_PW = 4 * GDN_W + LANES
SSD_PW = SSD_CONV_CH + SSD_W + LANES
RET_PW = 4 * RET_W
GDN_ALPHA_LANE = GDN_HEADS

VMEM_LIMIT_BYTES = 63 * 1024 * 1024


def _mm(a, b):
    return jnp.dot(a.astype(BF16), b.astype(BF16), preferred_element_type=F32)


def _mm_nt(a, b):
    return lax.dot_general(a.astype(BF16), b.astype(BF16), (((1,), (1,)), ((), ())),
                           preferred_element_type=F32)


def _mm_tn(a, b):
    return lax.dot_general(a.astype(BF16), b.astype(BF16), (((0,), (0,)), ((), ())),
                           preferred_element_type=F32)


def _split3(x):
    hi = x.astype(BF16)
    r = x - hi.astype(F32)
    mid = r.astype(BF16)
    lo = (r - mid.astype(F32)).astype(BF16)
    return hi, mid, lo


def _mm_sel(sel_tiled, x, pieces=3):
    return jnp.dot(sel_tiled, jnp.concatenate(_split3(x)[:pieces], axis=0), preferred_element_type=F32)


def _sigmoid(x):
    return 0.5 * jnp.tanh(0.5 * x) + 0.5


def _silu(x):
    return x * _sigmoid(x)


def _softplus(x):
    return jnp.maximum(x, 0.0) + jnp.log(1.0 + jnp.exp(-jnp.abs(x)))


def _rms(x, w):
    return x * lax.rsqrt(jnp.mean(x * x, axis=-1, keepdims=True) + EPS) * w


def _zip_rounds(*gens):
    gens = list(gens)
    while gens:
        alive = []
        for g in gens:
            try:
                next(g)
                alive.append(g)
            except StopIteration:
                pass
        gens = alive
        if gens:
            yield


_EXHAUSTED = object()


def _zip_ratio(main, side, n_main, n_side):
    issued, i = 0, 0
    for _ in main:
        i += 1
        while issued * n_main < i * n_side and next(side, _EXHAUSTED) is not _EXHAUSTED:
            issued += 1
        yield
    for _ in side:
        yield


def _run(gen):
    for _ in gen:
        pass


def _waves(n_waves, prep, phase_a, phase_b):
    prev = None
    for w in range(n_waves):
        probs = prep(w)
        res = {}
        if prev is None:
            yield from phase_a(probs, res)
        else:
            yield from _zip_rounds(phase_a(probs, res), phase_b(*prev))
        prev = (probs, res)
    yield from phase_b(*prev)


def _cols(p_ref, r0, c0, c1):
    tiles = [p_ref[j, r0:r0 + CHUNK, :] for j in range(c0 // LANES, c1 // LANES)]
    return tiles[0] if len(tiles) == 1 else jnp.concatenate(tiles, axis=1)


def _store_cols(p_ref, tb, c0, val):
    for o in range(0, val.shape[1], LANES):
        p_ref[(c0 + o) // LANES, CONV_PAD:CONV_PAD + tb, :] = val[:, o:o + LANES]


def _project_now(x, pn_ref, w_ref, p_ref, tb):
    h = _rms(x, pn_ref[...]).astype(BF16)
    _store_cols(p_ref, tb, 0, jnp.dot(h, w_ref[...], preferred_element_type=F32))


def _project_sliced(x_view, pn_ref, w_ref, h_ref, p_ref, tb):
    h_ref[...] = _rms(x_view[...], pn_ref[...]).astype(BF16)
    width = w_ref.shape[1]
    for j in range(0, width, PROJ_SLICE):
        wj = min(PROJ_SLICE, width - j)
        _store_cols(p_ref, tb, j, jnp.dot(h_ref[...], w_ref[:, j:j + wj], preferred_element_type=F32))
        yield


def _two_blocks(block_fn, n_groups, x_ref, xn_ref, pn_ref, w_ref, h_ref, p0_ref, p1_ref, tb):
    b, t = pl.program_id(0), pl.program_id(1)
    seq_start = t == 0
    n_slices = -(-w_ref.shape[1] // PROJ_SLICE)

    @pl.when(jnp.logical_and(b == 0, seq_start))
    def _():
        _project_now(x_ref[0:tb, :], pn_ref, w_ref, p0_ref, tb)

    @pl.when(seq_start)
    def _():
        p0_ref[:, 0:CONV_PAD, :] = jnp.zeros((p0_ref.shape[0], CONV_PAD, LANES), F32)

    @pl.when(jnp.logical_not(seq_start))
    def _():
        p0_ref[:, 0:CONV_PAD, :] = p1_ref[:, tb:tb + CONV_PAD, :]

    _run(_zip_ratio(block_fn(p0_ref, 0),
                    _project_sliced(x_ref.at[tb:2 * tb, :], pn_ref, w_ref, h_ref, p1_ref, tb), n_groups, n_slices))
    p1_ref[:, 0:CONV_PAD, :] = p0_ref[:, tb:tb + CONV_PAD, :]
    _run(_zip_ratio(block_fn(p1_ref, tb),
                    _project_sliced(xn_ref, pn_ref, w_ref, h_ref, p0_ref, tb), n_groups, n_slices))


def _conv_chunk(p_ref, base, width, cw):
    acc = _cols(p_ref, base + CONV_PAD, 0, width) * cw[CONV_W - 1:CONV_W]
    for i in range(CONV_W - 1):
        lo = CONV_PAD - (CONV_W - 1) + i
        acc = acc + _cols(p_ref, base + lo, 0, width) * cw[i:i + 1]
    return acc


def _tri_masks():
    row = lax.broadcasted_iota(jnp.int32, (CHUNK, CHUNK), 0)
    col = lax.broadcasted_iota(jnp.int32, (CHUNK, CHUNK), 1)
    return row >= col, row > col


def _tiled_ltri(pieces):
    row = lax.broadcasted_iota(jnp.int32, (CHUNK, pieces * CHUNK), 0)
    col = lax.broadcasted_iota(jnp.int32, (CHUNK, pieces * CHUNK), 1)
    return (row >= (col & (CHUNK - 1))).astype(BF16)


SOLVE_GROUPS = 7


def _solve_unit_lower(a_list, rhs_list):
    lane = lax.broadcasted_iota(jnp.int32, (CHUNK, 2 * CHUNK), 1)
    row = lax.broadcasted_iota(jnp.int32, (CHUNK, 2 * CHUNK), 0)
    left = lane < CHUNK
    zeros = jnp.zeros((CHUNK, 2 * CHUNK), F32)
    pts = [jnp.where(lane == row + CHUNK, 1.0, jnp.concatenate([-a, jnp.zeros_like(a)], axis=1)) for a in a_list]
    k = 1
    while k < CHUNK:
        news = [_mm(jnp.where(left, pt, 0.0), jnp.concatenate([pt, zeros], axis=0)) for pt in pts]
        yield
        pts = [new + jnp.where(left, 0.0, pt) for new, pt in zip(news, pts)]
        k *= 2
    sols = [_mm(jnp.where(left, 0.0, pt), jnp.concatenate([jnp.zeros_like(rhs), rhs], axis=0))
            for pt, rhs in zip(pts, rhs_list)]
    yield
    return sols


def _gdn_consts(cw_ref, alog_ref, dtb_ref, gn_ref):
    causal, strict = _tri_masks()
    return dict(causal=causal, strict=strict, ltri3=_tiled_ltri(3), cw=cw_ref[...],
                neg_a=-jnp.exp(alog_ref[...]), dtb=dtb_ref[...], gn=gn_ref[...])


def _gdn_groups(tb):
    return (1 + SOLVE_GROUPS) * (tb // (CHUNK * GDN_WAVE)) + 2 * GDN_WAVE


def _gdn_waves(p_ref, o_ref, out_row0, tb, state, k):
    heads = range(GDN_HEADS)
    causal, strict = k["causal"], k["strict"]

    def prep(w):
        probs = []
        for u in range(GDN_WAVE):
            base = (w * GDN_WAVE + u) * CHUNK
            r0 = base + CONV_PAD
            qkv = _silu(_conv_chunk(p_ref, base, GDN_CONV_CH, k["cw"]))
            z = _cols(p_ref, r0, GDN_CONV_CH, GDN_CONV_CH + GDN_W)
            gt = _cols(p_ref, r0, 4 * GDN_W, GDN_PW)
            beta = _sigmoid(gt)
            g = k["neg_a"] * _softplus(gt + k["dtb"])
            gcum = _mm_sel(k["ltri3"], g)
            gcum_t = gcum.T
            egc = jnp.exp(gcum)
            for h in heads:
                al = GDN_ALPHA_LANE + h
                q = qkv[:, h * GDN_D:(h + 1) * GDN_D]
                kk = qkv[:, GDN_W + h * GDN_D:GDN_W + (h + 1) * GDN_D]
                v = qkv[:, 2 * GDN_W + h * GDN_D:2 * GDN_W + (h + 1) * GDN_D]
                q = q * (lax.rsqrt(jnp.sum(q * q, axis=-1, keepdims=True) + EPS) * (GDN_D ** -0.5))
                kk = kk * lax.rsqrt(jnp.sum(kk * kk, axis=-1, keepdims=True) + EPS)
                bt = beta[:, h:h + 1]
                gc = gcum[:, al:al + 1]
                eg = egc[:, al:al + 1]
                gl = gcum[CHUNK - 1:CHUNK, al:al + 1]
                kb = kk * bt
                probs.append(dict(
                    row=out_row0 + base, q=q, k=kk, kb=kb, qe=q * eg,
                    dec=jnp.exp(jnp.where(causal, gc - gcum_t[al:al + 1, :], -jnp.inf)),
                    rhs=jnp.concatenate([v * bt, kb * eg], axis=1),
                    kdec=kk * jnp.exp(gl - gc), egl=jnp.exp(gl),
                    zh=z[:, h * GDN_D:(h + 1) * GDN_D]))
        return probs

    def phase_a(probs, res):
        kk = [_mm_nt(p["kb"], p["k"]) for p in probs]
        qk = [_mm_nt(p["q"], p["k"]) for p in probs]
        yield
        res["sols"] = yield from _solve_unit_lower(
            [jnp.where(strict, m * p["dec"], 0.0) for m, p in zip(kk, probs)], [p["rhs"] for p in probs])
        res["attn"] = [m * p["dec"] for m, p in zip(qk, probs)]

    def phase_b(probs, res):
        for u in range(GDN_WAVE):
            sl = slice(u * GDN_HEADS, (u + 1) * GDN_HEADS)
            ps, so, at = probs[sl], res["sols"][sl], res["attn"][sl]
            ws = [_mm(so[h][:, GDN_D:], state[h]) for h in heads]
            qs = [_mm(ps[h]["qe"], state[h]) for h in heads]
            yield
            v_new = [so[h][:, :GDN_D] - ws[h] for h in heads]
            av = [_mm(at[h], v_new[h]) for h in heads]
            kv = [_mm_tn(ps[h]["kdec"], v_new[h]) for h in heads]
            yield
            for h in heads:
                state[h] = state[h] * ps[h]["egl"] + kv[h]
                o_ref[ps[h]["row"]:ps[h]["row"] + CHUNK, h * GDN_D:(h + 1) * GDN_D] = (
                    _rms(qs[h] + av[h], k["gn"]) * _silu(ps[h]["zh"])).astype(o_ref.dtype)

    return _waves(tb // (CHUNK * GDN_WAVE), prep, phase_a, phase_b)


def _ret_groups(tb):
    return 2 * (tb // (CHUNK * RET_WAVE)) + RET_WAVE


def _ret_waves(p_ref, c0, o_ref, out_row0, tb, state, k):
    heads = range(RET_HEADS)

    def prep(w):
        probs = []
        for u in range(RET_WAVE):
            base = (w * RET_WAVE + u) * CHUNK
            r0 = base + CONV_PAD
            cosf = k["cos"][out_row0 + base:out_row0 + base + CHUNK, :]
            sinf = k["sin"][out_row0 + base:out_row0 + base + CHUNK, :]
            for h in heads:
                hs = slice(h * RET_D, (h + 1) * RET_D)
                q = _cols(p_ref, r0, c0 + h * RET_D, c0 + (h + 1) * RET_D)
                kk = _cols(p_ref, r0, c0 + RET_W + h * RET_D, c0 + RET_W + (h + 1) * RET_D)
                q = q * cosf + pltpu.roll(q, RET_D // 2, 1) * sinf
                kk = (kk * cosf + pltpu.roll(kk, RET_D // 2, 1) * sinf) * (RET_D ** -0.5)
                probs.append(dict(
                    row=out_row0 + base, h=h, hs=hs, q=q, k=kk, kd=kk * k["kdec"][:, hs],
                    v=_cols(p_ref, r0, c0 + 2 * RET_W + h * RET_D, c0 + 2 * RET_W + (h + 1) * RET_D),
                    gate=_cols(p_ref, r0, c0 + 3 * RET_W + h * RET_D, c0 + 3 * RET_W + (h + 1) * RET_D)))
        return probs

    def phase_a(probs, res):
        scores = [_mm_nt(p["q"], p["k"]) * k["dmat"][p["h"]] for p in probs]
        res["kv"] = [_mm_tn(p["kd"], p["v"]) for p in probs]
        yield
        res["intra"] = [_mm(sc, p["v"]) for sc, p in zip(scores, probs)]
        yield

    def phase_b(probs, res):
        for u in range(RET_WAVE):
            sl = slice(u * RET_HEADS, (u + 1) * RET_HEADS)
            ps, kv, intra = probs[sl], res["kv"][sl], res["intra"][sl]
            qr = [_mm(ps[h]["q"], state[h]) for h in heads]
            yield
            for h in heads:
                hs = ps[h]["hs"]
                state[h] = state[h] * k["cdec"][:, hs] + kv[h]
                o = intra[h] + qr[h] * k["qdec"][:, hs]
                o_ref[ps[h]["row"]:ps[h]["row"] + CHUNK, hs] = (
                    _rms(o, k["nrm"]) * _silu(ps[h]["gate"])).astype(o_ref.dtype)

    return _waves(tb // (CHUNK * RET_WAVE), prep, phase_a, phase_b)


def _gdn_ret_kernel(x_ref, xn_ref, pn_ref, w_ref, cw_ref, alog_ref, dtb_ref, gn_ref,
                    cos_ref, sin_ref, dmat_ref, qdec_ref, kdec_ref, cdec_ref, rn_ref,
                    oa_ref, oc_ref, h_ref, p0_ref, p1_ref, s_ref, r_ref, *, tb):
    @pl.when(pl.program_id(1) == 0)
    def _():
        s_ref[...] = jnp.zeros_like(s_ref)
        r_ref[...] = jnp.zeros_like(r_ref)

    kg = _gdn_consts(cw_ref, alog_ref, dtb_ref, gn_ref)
    kr = dict(cos=cos_ref, sin=sin_ref, dmat=dmat_ref, qdec=qdec_ref, kdec=kdec_ref, cdec=cdec_ref,
              nrm=rn_ref[...])
    s_state = [s_ref[h] for h in range(GDN_HEADS)]
    r_state = [r_ref[h] for h in range(RET_HEADS)]

    def block(p_ref, out_row0):
        return _zip_ratio(_gdn_waves(p_ref, oa_ref, out_row0, tb, s_state, kg),
                          _ret_waves(p_ref, GDN_PW, oc_ref, out_row0, tb, r_state, kr),
                          _gdn_groups(tb), _ret_groups(tb))

    _two_blocks(block, _gdn_groups(tb), x_ref, xn_ref, pn_ref, w_ref, h_ref, p0_ref, p1_ref, tb)
    for h in range(GDN_HEADS):
        s_ref[h] = s_state[h]
    for h in range(RET_HEADS):
        r_ref[h] = r_state[h]


def _ssd_kernel(x_ref, xn_ref, pn_ref, w_ref, cw_ref, cb_ref, alog_ref, dtb_ref, dx_ref, nrm_ref, o_ref,
                h_ref, p0_ref, p1_ref, hs_ref, ac_ref, dt_ref, *, tb):
    @pl.when(pl.program_id(1) == 0)
    def _():
        hs_ref[...] = jnp.zeros_like(hs_ref)

    row = lax.broadcasted_iota(jnp.int32, (CHUNK, LANES), 0)
    lane = lax.broadcasted_iota(jnp.int32, (CHUNK, LANES), 1)
    causal2 = row >= (lane & (CHUNK - 1))
    even_head = lane < SSD_P
    ltri3 = _tiled_ltri(3)
    cw = cw_ref[...]
    cbias = cb_ref[...]
    neg_a = -jnp.exp(alog_ref[...])
    dtb = dtb_ref[...]
    groups = range(SSD_GROUPS)
    pairs_per_group = SSD_HEADS // SSD_GROUPS // 2
    state = [hs_ref[g] for g in groups]
    n_waves = tb // (CHUNK * SSD_WAVE)

    def block(p_ref, out_row0):
        def cumulative():
            for c in range(tb // CHUNK):
                dt = _softplus(_cols(p_ref, c * CHUNK + CONV_PAD, SSD_CONV_CH + SSD_W, SSD_PW) + dtb)
                dt_ref[c * CHUNK:(c + 1) * CHUNK, :] = dt
                ac_ref[c * CHUNK:(c + 1) * CHUNK, :] = _mm_sel(ltri3, dt * neg_a)
            yield

        def per_channel(t):
            return [jnp.where(even_head, t[:, 2 * m:2 * m + 1], t[:, 2 * m + 1:2 * m + 2])
                    for m in range(SSD_HEADS // 2)]

        def prep(w):
            chunks = []
            for u in range(SSD_WAVE):
                base = (w * SSD_WAVE + u) * CHUNK
                xbc = _silu(_conv_chunk(p_ref, base, SSD_CONV_CH, cw) + cbias)
                xs = xbc[:, :SSD_W]
                acum = ac_ref[base:base + CHUNK, :]
                ac_cols = per_channel(acum)
                pair_t = jnp.concatenate([acum, pltpu.roll(acum, LANES - 1, 1)], axis=0).T
                chunks.append(dict(
                    row=out_row0 + base, base=base, xs=xs, ac_cols=ac_cols, pair_t=pair_t,
                    ac_x=jnp.concatenate(ac_cols, axis=1),
                    xdt=xs * jnp.concatenate(per_channel(dt_ref[base:base + CHUNK, :]), axis=1),
                    bg=[xbc[:, SSD_W + g * SSD_N:SSD_W + (g + 1) * SSD_N] for g in groups],
                    cg=[xbc[:, SSD_W + (SSD_GROUPS + g) * SSD_N:SSD_W + (SSD_GROUPS + g + 1) * SSD_N]
                        for g in groups],
                    z=_cols(p_ref, base + CONV_PAD, SSD_CONV_CH, SSD_CONV_CH + SSD_W)))
            return chunks

        def pair_product(c, cb2, m):
            lmat = jnp.exp(jnp.where(causal2, c["ac_cols"][m] - c["pair_t"][2 * m:2 * m + 1, :], -jnp.inf))
            x2 = c["xdt"][:, m * LANES:(m + 1) * LANES]
            rhs = jnp.concatenate([jnp.where(even_head, x2, 0.0), jnp.where(even_head, 0.0, x2)], axis=0)
            return _mm(cb2[m // pairs_per_group] * lmat, rhs)

        def phase_a(chunks, res):
            cb2s = [[_mm_nt(c["cg"][g], jnp.concatenate([c["bg"][g], c["bg"][g]], axis=0)) for g in groups]
                    for c in chunks]
            yield
            halves = []
            for half in range(2):
                ms = range(half * pairs_per_group, (half + 1) * pairs_per_group)
                halves.append([[pair_product(c, cb2, m) for m in ms] for c, cb2 in zip(chunks, cb2s)])
                yield
            res["intra"] = [jnp.concatenate(halves[0][i] + halves[1][i], axis=1) for i in range(len(chunks))]

        def phase_b(chunks, res):
            for c, y_intra in zip(chunks, res["intra"]):
                ac_x = c["ac_x"]
                eac_x = jnp.exp(ac_x)
                al_x = ac_x[CHUNK - 1:CHUNK, :]
                wx = jnp.exp(al_x - ac_x) * c["xdt"]
                inter = [_mm(c["cg"][g], state[g]) for g in groups]
                upd = [_mm_tn(c["bg"][g], wx[:, g * SSD_GW:(g + 1) * SSD_GW]) for g in groups]
                yield
                y = (y_intra + jnp.concatenate(inter, axis=1) * eac_x + c["xs"] * dx_ref[...]) * _silu(c["z"])
                for g in groups:
                    gs = slice(g * SSD_GW, (g + 1) * SSD_GW)
                    state[g] = state[g] * jnp.exp(al_x[:, gs]) + upd[g]
                    o_ref[c["row"]:c["row"] + CHUNK, gs] = _rms(y[:, gs], nrm_ref[:, gs]).astype(o_ref.dtype)

        def pipeline():
            yield from cumulative()
            yield from _waves(n_waves, prep, phase_a, phase_b)

        return pipeline()

    n_groups = 1 + 3 * n_waves + SSD_WAVE
    _two_blocks(block, n_groups, x_ref, xn_ref, pn_ref, w_ref, h_ref, p0_ref, p1_ref, tb)
    for g in groups:
        hs_ref[g] = state[g]


def _out_kernel(x_ref, a_ref, b_ref, c_ref, wa_ref, wb_ref, wc_ref, pn_ref, o_ref):
    out = (jnp.dot(a_ref[...], wa_ref[...], preferred_element_type=F32)
           + jnp.dot(b_ref[...], wb_ref[...], preferred_element_type=F32)
           + jnp.dot(c_ref[...], wc_ref[...], preferred_element_type=F32))
    o_ref[...] = x_ref[...] + _rms(out, pn_ref[...])


def _pad_lanes(t):
    return jnp.pad(t, ((0, 0), (0, LANES - t.shape[-1])))


def _split(t, sizes):
    out, start = [], 0
    for s in sizes:
        out.append(t[..., start:start + s])
        start += s
    return out


def _const_spec(shape):
    return pl.BlockSpec(shape, lambda b, t: (0,) * len(shape))


def _mixer_call(body, x, consts, const_specs, pw, out_ws, scratch, tb):
    bsz, seq, _ = x.shape
    steps = seq // (2 * tb)

    def next_block(b, t):
        last = t == steps - 1
        return (jnp.where(last, jnp.minimum(b + 1, bsz - 1), b), jnp.where(last, 0, 2 * (t + 1)), 0)

    in_specs = [pl.BlockSpec((None, 2 * tb, D_MODEL), lambda b, t: (b, t, 0)),
                pl.BlockSpec((None, tb, D_MODEL), next_block)] + const_specs
    return pl.pallas_call(
        functools.partial(body, tb=tb),
        grid=(bsz, steps),
        in_specs=in_specs,
        out_specs=[pl.BlockSpec((None, 2 * tb, w), lambda b, t: (b, t, 0)) for w in out_ws],
        out_shape=[jax.ShapeDtypeStruct((bsz, seq, w), BF16) for w in out_ws],
        scratch_shapes=[pltpu.VMEM((tb, D_MODEL), BF16),
                        pltpu.VMEM((pw // LANES, tb + CONV_PAD, LANES), F32),
                        pltpu.VMEM((pw // LANES, tb + CONV_PAD, LANES), F32), *scratch],
        compiler_params=pltpu.CompilerParams(dimension_semantics=("arbitrary", "arbitrary"),
                                             vmem_limit_bytes=VMEM_LIMIT_BYTES),
    )(x, x, *consts)


def _retention_tables(seq):
    half = RET_D // 2
    pos = jnp.arange(seq, dtype=jnp.int32)
    inv = ROPE_BASE ** (-jnp.arange(half, dtype=F32) / half)
    ang = pos.astype(F32)[:, None] * inv[None, :]
    cos, sin = jnp.cos(ang), jnp.sin(ang)
    cosf = jnp.concatenate([cos, cos], axis=-1)
    sinf = jnp.concatenate([-sin, sin], axis=-1)
    lg = jnp.log(1.0 - 2.0 ** (-5.0 - jnp.arange(RET_HEADS, dtype=F32)))
    idx = jnp.arange(CHUNK, dtype=F32)
    rel = idx[:, None] - idx[None, :]
    dmat = jnp.where(rel[None] >= 0, jnp.exp(jnp.maximum(rel, 0.0)[None] * lg[:, None, None]), 0.0)
    qdec = jnp.exp((idx[:, None] + 1.0) * lg[None, :])
    kdec = jnp.exp((CHUNK - 1.0 - idx)[:, None] * lg[None, :])
    cdec = jnp.exp(CHUNK * lg)
    rep = lambda t: jnp.repeat(t, RET_D, axis=-1)
    return cosf, sinf, dmat, rep(qdec), rep(kdec), rep(cdec[None, :])


def _layer(x, pre_norm, post_norm, w_in, gdn_conv, gdn_A_log, gdn_dt_bias, gdn_norm, ssd_conv, ssd_conv_b,
           ssd_A_log, ssd_dt_bias, ssd_D, ssd_norm, ret_norm, w_out, tables, tb, tm):
    bsz, seq, _ = x.shape
    (gq, gk, gv, gz, gb, ga, sx, sb, sc, sz, sdt, rq, rk, rv, rg) = _split(w_in, IN_SIZES)
    w_g = jnp.concatenate([gq, gk, gv, gz, _pad_lanes(jnp.concatenate([gb, ga], axis=-1))], axis=-1).astype(BF16)
    w_s = jnp.concatenate([sx, sb, sc, sz, _pad_lanes(sdt)], axis=-1).astype(BF16)
    w_r = jnp.concatenate([rq, rk, rv, rg], axis=-1).astype(BF16)
    pn = pre_norm[None, :]
    cs = _const_spec

    g_alog = _pad_lanes(jnp.concatenate([jnp.zeros((GDN_HEADS,), F32), gdn_A_log])[None, :])
    g_dtb = _pad_lanes(jnp.concatenate([jnp.zeros((GDN_HEADS,), F32), gdn_dt_bias])[None, :])
    cosf, sinf, dmat, qdec, kdec, cdec = tables
    rope_spec = pl.BlockSpec((2 * tb, RET_D), lambda b, t: (t, 0))
    once = dict(pipeline_mode=pl.Buffered(1))
    o_a, o_c = _mixer_call(
        _gdn_ret_kernel, x,
        [pn, jnp.concatenate([w_g, w_r], axis=-1), gdn_conv, g_alog, g_dtb, gdn_norm[None, :],
         cosf, sinf, dmat, qdec, kdec, cdec, ret_norm[None, :]],
        [cs((1, D_MODEL)), pl.BlockSpec((D_MODEL, GDN_PW + RET_PW), lambda b, t: (0, 0), **once),
         cs((CONV_W, GDN_CONV_CH)), cs((1, LANES)), cs((1, LANES)), cs((1, GDN_D)),
         rope_spec, rope_spec, cs((RET_HEADS, CHUNK, CHUNK)), cs((CHUNK, RET_W)), cs((CHUNK, RET_W)),
         cs((1, RET_W)), cs((1, RET_D))],
        GDN_PW + RET_PW, [GDN_W, RET_W],
        [pltpu.VMEM((GDN_HEADS, GDN_D, GDN_D), F32), pltpu.VMEM((RET_HEADS, RET_D, RET_D), F32)], tb)

    (o_b,) = _mixer_call(
        _ssd_kernel, x,
        [pn, w_s, ssd_conv, ssd_conv_b[None, :], _pad_lanes(ssd_A_log[None, :]), _pad_lanes(ssd_dt_bias[None, :]),
         jnp.repeat(ssd_D, SSD_P)[None, :], ssd_norm[None, :]],
        [cs((1, D_MODEL)), pl.BlockSpec((D_MODEL, SSD_PW), lambda b, t: (0, 0), **once),
         cs((CONV_W, SSD_CONV_CH)), cs((1, SSD_CONV_CH)), cs((1, LANES)), cs((1, LANES)), cs((1, SSD_W)),
         cs((1, SSD_W))],
        SSD_PW, [SSD_W],
        [pltpu.VMEM((SSD_GROUPS, SSD_N, SSD_GW), F32), pltpu.VMEM((tb, LANES), F32), pltpu.VMEM((tb, LANES), F32)],
        tb)

    n_tok = bsz * seq
    w_o = w_out.astype(BF16)
    tok = lambda w: pl.BlockSpec((tm, w), lambda i: (i, 0))
    full = lambda r, w: pl.BlockSpec((r, w), lambda i: (0, 0))
    y = pl.pallas_call(
        _out_kernel,
        grid=(n_tok // tm,),
        in_specs=[tok(D_MODEL), tok(GDN_W), tok(SSD_W), tok(RET_W),
                  full(GDN_W, D_MODEL), full(SSD_W, D_MODEL), full(RET_W, D_MODEL), full(1, D_MODEL)],
        out_specs=tok(D_MODEL),
        out_shape=jax.ShapeDtypeStruct((n_tok, D_MODEL), F32),
        compiler_params=pltpu.CompilerParams(dimension_semantics=("arbitrary",),
                                             vmem_limit_bytes=VMEM_LIMIT_BYTES),
    )(x.reshape(n_tok, D_MODEL), o_a.reshape(n_tok, GDN_W), o_b.reshape(n_tok, SSD_W),
      o_c.reshape(n_tok, RET_W), w_o[:GDN_W], w_o[GDN_W:GDN_W + SSD_W], w_o[GDN_W + SSD_W:], post_norm[None, :])
    return y.reshape(bsz, seq, D_MODEL)


def kernel(x, pre_norm, post_norm, w_in, gdn_conv, gdn_A_log, gdn_dt_bias, gdn_norm, ssd_conv, ssd_conv_b,
           ssd_A_log, ssd_dt_bias, ssd_D, ssd_norm, ret_norm, w_out):
    seq = x.shape[1]
    tb = min(512, seq // 2)
    tm = min(1024, x.shape[0] * seq)
    tables = _retention_tables(seq)
    for l in range(pre_norm.shape[0]):
        x = _layer(x, pre_norm[l], post_norm[l], w_in[l], gdn_conv[l], gdn_A_log[l], gdn_dt_bias[l], gdn_norm[l],
                   ssd_conv[l], ssd_conv_b[l], ssd_A_log[l], ssd_dt_bias[l], ssd_D[l], ssd_norm[l], ret_norm[l],
                   w_out[l], tables, tb, tm)
    return x
```
